```python
import math
import jax, jax.numpy as jnp
from jax import lax
import numpy as np

D_MODEL = 2048
BATCH = 1
SEQ = 8192
DEPTH = 4

GRID_W = 64
CTX_LEN = 256
HEAD_DIM = 128
NA_HEADS = 8
NA_WIDTH = NA_HEADS * HEAD_DIM
NA_KH_MAX = 8
NA_KW = 16
S5_WIDTH = 1024
S5_GROUP = 16
S5_GROUPS = S5_WIDTH // S5_GROUP
S5_STATE = 64
DT_MIN = 1e-3
DT_MAX = 1e-1
GQA_HEADS = 16
GQA_KV_HEADS = 4
GQA_WIDTH = GQA_HEADS * HEAD_DIM
GQA_KV_WIDTH = GQA_KV_HEADS * HEAD_DIM
ROPE_THETA = 10000.0
Q_BLOCK = 128
EVEN_IN = 4 * NA_WIDTH + 2 * S5_WIDTH
EVEN_OUT = NA_WIDTH + S5_WIDTH
ODD_IN = 2 * GQA_WIDTH + 2 * GQA_KV_WIDTH
DN_ALPHA = (2 * DEPTH) ** 0.25
DN_BETA = (8 * DEPTH) ** -0.25
LN_EPS = 1e-6
RMS_EPS = 1e-6

kernel_name = "hybrid_natten_s5_gqa_flow_backbone"


def layer_norm(x, g, b):
    xf = x.astype(jnp.float32)
    mu = jnp.mean(xf, -1, keepdims=True)
    var = jnp.mean(jnp.square(xf - mu), -1, keepdims=True)
    return ((xf - mu) * lax.rsqrt(var + LN_EPS) * g + b).astype(x.dtype)


def rms_norm(x, g):
    xf = x.astype(jnp.float32)
    return (xf * lax.rsqrt(jnp.mean(xf * xf, -1, keepdims=True) + RMS_EPS) * g).astype(x.dtype)


def ada_mod(cvec, w, b):
    m = jax.nn.silu(cvec) @ w + b
    return jnp.split(m, 3, -1)


def axial_rope(x):
    n = x.shape[1]
    t = jnp.arange(n)
    row = (t // GRID_W).astype(jnp.float32)
    col = (t % GRID_W).astype(jnp.float32)
    half = HEAD_DIM // 2
    per_axis = half // 2
    inv = ROPE_THETA ** (-jnp.arange(per_axis, dtype=jnp.float32) / per_axis)
    ang = jnp.concatenate([row[:, None] * inv, col[:, None] * inv], -1)
    cos = jnp.cos(ang)[None, :, None, :]
    sin = jnp.sin(ang)[None, :, None, :]
    xf = x.astype(jnp.float32)
    x1, x2 = xf[..., :half], xf[..., half:]
    return jnp.concatenate([x1 * cos - x2 * sin, x2 * cos + x1 * sin], -1).astype(x.dtype)


def gqa_attention(q, k, v):
    b, t, hq, dh = q.shape
    hkv = k.shape[2]
    qg = q.reshape(b, t, hkv, hq // hkv, dh)
    s = jnp.einsum('btkgd,bskd->bkgts', qg, k).astype(jnp.float32) * (dh ** -0.5)
    p = jax.nn.softmax(s, -1).astype(v.dtype)
    o = jnp.einsum('bkgts,bskd->btkgd', p, v)
    return o.reshape(b, t, hq, dh)


def neighbourhood_attention(q, k, v, qc, kc, vc, rpb, need_ctx):
    b, n, _ = q.shape
    rows = n // GRID_W
    kh = min(NA_KH_MAX, rows)
    scale = HEAD_DIM ** -0.5
    qg = q.reshape(b, rows, GRID_W, NA_HEADS, HEAD_DIM)
    kg = k.reshape(b, rows, GRID_W, NA_HEADS, HEAD_DIM)
    vg = v.reshape(b, rows, GRID_W, NA_HEADS, HEAD_DIM)
    kcx = kc.reshape(b, -1, NA_HEADS, HEAD_DIM)
    vcx = vc.reshape(b, -1, NA_HEADS, HEAD_DIM)
    r_start = jnp.clip(jnp.arange(rows) - kh // 2, 0, rows - kh)
    c_idx = jnp.arange(GRID_W)
    c_start = jnp.clip(c_idx - NA_KW // 2, 0, GRID_W - NA_KW)
    col_nb = c_start[:, None] + jnp.arange(NA_KW)
    dc = col_nb - c_idx[:, None] + NA_KW - 1
    n_loc = kh * NA_KW

    def one_row(r):
        rr = r_start[r] + jnp.arange(kh)
        k_nb = jnp.take(jnp.take(kg, rr, axis=1), col_nb, axis=2)
        v_nb = jnp.take(jnp.take(vg, rr, axis=1), col_nb, axis=2)
        dr = rr - r + NA_KH_MAX - 1
        bias = jnp.take(jnp.take(rpb, dr, axis=1), dc, axis=2)
        bias = bias.transpose(0, 2, 1, 3)
        qr = qg[:, r]
        s_loc = jnp.einsum('bwhd,bawkhd->bhwak', qr, k_nb).astype(jnp.float32) * scale + bias[None].astype(jnp.float32)
        s_ctx = jnp.einsum('bwhd,bchd->bhwc', qr, kcx).astype(jnp.float32) * scale
        s = jnp.concatenate([s_loc.reshape(b, NA_HEADS, GRID_W, n_loc), s_ctx], -1)
        p = jax.nn.softmax(s, -1).astype(v.dtype)
        p_loc = p[..., :n_loc].reshape(b, NA_HEADS, GRID_W, kh, NA_KW)
        p_ctx = p[..., n_loc:]
        return (jnp.einsum('bhwak,bawkhd->bwhd', p_loc, v_nb)
                + jnp.einsum('bhwc,bchd->bwhd', p_ctx, vcx))

    o = lax.map(one_row, jnp.arange(rows))
    o = o.transpose(1, 0, 2, 3, 4).reshape(b, n, NA_WIDTH)
    oc = None
    if need_ctx:
        qcx = qc.reshape(b, -1, NA_HEADS, HEAD_DIM)
        oc = gqa_attention(qcx, kcx, vcx).reshape(b, -1, NA_WIDTH)
    return o, oc


def s5_discretize(a_re, a_im, log_dt, b_re, b_im):
    a_re = a_re.astype(jnp.float32); a_im = a_im.astype(jnp.float32)
    b_re = b_re.astype(jnp.float32); b_im = b_im.astype(jnp.float32)
    dt = jnp.exp(log_dt.astype(jnp.float32))[..., None]
    mag = jnp.exp(a_re * dt)
    lb_re = mag * jnp.cos(a_im * dt)
    lb_im = mag * jnp.sin(a_im * dt)
    den = a_re * a_re + a_im * a_im
    n_re, n_im = lb_re - 1.0, lb_im
    f_re = (n_re * a_re + n_im * a_im) / den
    f_im = (n_im * a_re - n_re * a_im) / den
    bb_re = f_re[..., None] * b_re - f_im[..., None] * b_im
    bb_im = f_re[..., None] * b_im + f_im[..., None] * b_re
    return lb_re, lb_im, bb_re, bb_im


def complex_linear_scan(a_re, a_im, b_re, b_im, reverse):
    ar = jnp.broadcast_to(a_re, b_re.shape)
    ai = jnp.broadcast_to(a_im, b_re.shape)

    def combine(e1, e2):
        a1r, a1i, b1r, b1i = e1
        a2r, a2i, b2r, b2i = e2
        return (a1r * a2r - a1i * a2i, a1r * a2i + a1i * a2r,
                a2r * b1r - a2i * b1i + b2r, a2r * b1i + a2i * b1r + b2i)

    _, _, hr, hi = lax.associative_scan(combine, (ar, ai, b_re, b_im), reverse=reverse, axis=1)
    return hr, hi


def s5_bidirectional(u, uc, a_re, a_im, log_dt, b_re, b_im, c_re, c_im, d, need_ctx):
    lb_re, lb_im, bb_re, bb_im = s5_discretize(a_re, a_im, log_dt, b_re, b_im)
    c_re = c_re.astype(jnp.float32); c_im = c_im.astype(jnp.float32)
    df = d.astype(jnp.float32)
    uf = u.astype(jnp.float32)
    ucf = uc.astype(jnp.float32)

    def drive(z, k):
        zg = z.reshape(z.shape[0], z.shape[1], S5_GROUPS, S5_GROUP)
        return (jnp.einsum('blgc,gpc->blgp', zg, bb_re[k]),
                jnp.einsum('blgc,gpc->blgp', zg, bb_im[k]))

    def readout(hr, hi, k):
        y = jnp.einsum('blgp,gcp->blgc', hr, c_re[k]) - jnp.einsum('blgp,gcp->blgc', hi, c_im[k])
        return y.reshape(y.shape[0], y.shape[1], S5_WIDTH)

    y = uf * df
    yc = ucf * df
    for k, rev in ((0, False), (1, True)):
        brc, bic = drive(ucf, k)
        hrc, hic = complex_linear_scan(lb_re[k], lb_im[k], brc, bic, rev)
        end = 0 if rev else -1
        h0r, h0i = hrc[:, end], hic[:, end]
        br, bi = drive(uf, k)
        start = -1 if rev else 0
        br = br.at[:, start].add(lb_re[k] * h0r - lb_im[k] * h0i)
        bi = bi.at[:, start].add(lb_re[k] * h0i + lb_im[k] * h0r)
        hr, hi = complex_linear_scan(lb_re[k], lb_im[k], br, bi, rev)
        y = y + readout(hr, hi, k)
        if need_ctx:
            yc = yc + readout(hrc, hic, k)
    return y.astype(u.dtype), (yc.astype(uc.dtype) if need_ctx else None)


def even_layer(h, hc, w_in, w_out, rpb, a_re, a_im, log_dt, b_re, b_im, c_re, c_im, d, glu_w, glu_b, need_ctx):
    splits = [NA_WIDTH, 2 * NA_WIDTH, 3 * NA_WIDTH, 4 * NA_WIDTH, 4 * NA_WIDTH + S5_WIDTH]
    qa, ka, va, ga, ub, gb = jnp.split(h @ w_in, splits, -1)
    qac, kac, vac, gac, ubc, gbc = jnp.split(hc @ w_in, splits, -1)
    ya, yac = neighbourhood_attention(qa, ka, va, qac, kac, vac, rpb, need_ctx)
    yb, ybc = s5_bidirectional(ub, ubc, a_re, a_im, log_dt, b_re, b_im, c_re, c_im, d, need_ctx)

    def merge(za, zb, g_a, g_b):
        zb = jax.nn.gelu(zb)
        zb = zb * jax.nn.sigmoid(zb @ glu_w + glu_b)
        return jnp.concatenate([za * jax.nn.silu(g_a), zb * jax.nn.silu(g_b)], -1) @ w_out

    y = merge(ya, yb, ga, gb)
    yc = merge(yac, ybc, gac, gbc) if need_ctx else None
    return y, yc


def odd_layer(h, hc, w_in, w_out, q_g, k_g, need_ctx):
    b, n, _ = h.shape
    splits = [GQA_WIDTH, GQA_WIDTH + GQA_KV_WIDTH, GQA_WIDTH + 2 * GQA_KV_WIDTH]

    def project(z):
        bb, t, _ = z.shape
        q, k, v, g = jnp.split(z @ w_in, splits, -1)
        q = rms_norm(q.reshape(bb, t, GQA_HEADS, HEAD_DIM), q_g)
        k = rms_norm(k.reshape(bb, t, GQA_KV_HEADS, HEAD_DIM), k_g)
        v = v.reshape(bb, t, GQA_KV_HEADS, HEAD_DIM)
        return q, k, v, g

    q, k, v, g = project(h)
    qc, kc, vc, gc = project(hc)
    q = axial_rope(q)
    k = axial_rope(k)
    k_all = jnp.concatenate([k, kc], 1)
    v_all = jnp.concatenate([v, vc], 1)
    qb = q.reshape(b, n // Q_BLOCK, Q_BLOCK, GQA_HEADS, HEAD_DIM).transpose(1, 0, 2, 3, 4)
    o = lax.map(lambda qq: gqa_attention(qq, k_all, v_all), qb)
    o = o.transpose(1, 0, 2, 3, 4).reshape(b, n, GQA_WIDTH)
    y = (o * jax.nn.silu(g)) @ w_out
    yc = None
    if need_ctx:
        oc = gqa_attention(qc, kc, vc).reshape(b, hc.shape[1], GQA_WIDTH)
        yc = (oc * jax.nn.silu(gc)) @ w_out
    return y, yc


def setup_inputs(seed: int = 0) -> dict:
    key = jax.random.key(seed)
    ne = (DEPTH + 1) // 2
    no = DEPTH // 2
    f32 = jnp.float32

    def nrm(i, shape, s):
        return jax.random.normal(jax.random.fold_in(key, i), shape, f32) * s

    even_col = jnp.ones((EVEN_IN,), f32).at[2 * NA_WIDTH:3 * NA_WIDTH].set(DN_BETA)
    odd_col = jnp.ones((ODD_IN,), f32).at[GQA_WIDTH + GQA_KV_WIDTH:GQA_WIDTH + 2 * GQA_KV_WIDTH].set(DN_BETA)
    a_im0 = math.pi * jnp.arange(S5_STATE, dtype=f32)
    return {
        "x": nrm(0, (BATCH, SEQ, D_MODEL), 1.0),
        "c": nrm(1, (BATCH, D_MODEL), 1.0),
        "ctx": nrm(2, (BATCH, CTX_LEN, D_MODEL), 1.0),
        "c_ctx": nrm(3, (D_MODEL,), 1.0),
        "ada_w": nrm(4, (DEPTH, D_MODEL, 3 * D_MODEL), 0.5 * D_MODEL ** -0.5),
        "ada_b": nrm(5, (DEPTH, 3 * D_MODEL), 0.02),
        "ln_g": 1.0 + nrm(6, (DEPTH, D_MODEL), 0.02),
        "ln_b": nrm(7, (DEPTH, D_MODEL), 0.02),
        "ev_w_in": nrm(8, (ne, D_MODEL, EVEN_IN), D_MODEL ** -0.5) * even_col,
        "ev_w_out": nrm(9, (ne, EVEN_OUT, D_MODEL), DN_BETA * EVEN_OUT ** -0.5),
        "na_rpb": nrm(10, (ne, NA_HEADS, 2 * NA_KH_MAX - 1, 2 * NA_KW - 1), 0.02),
        "s5_a_re": -0.5 + nrm(11, (ne, 2, S5_GROUPS, S5_STATE), 0.01),
        "s5_a_im": a_im0 + nrm(12, (ne, 2, S5_GROUPS, S5_STATE), 0.01),
        "s5_log_dt": jax.random.uniform(jax.random.fold_in(key, 13), (ne, 2, S5_GROUPS), f32,
                                        minval=math.log(DT_MIN), maxval=math.log(DT_MAX)),
        "s5_b_re": nrm(14, (ne, 2, S5_GROUPS, S5_STATE, S5_GROUP), (2 * S5_GROUP) ** -0.5),
        "s5_b_im": nrm(15, (ne, 2, S5_GROUPS, S5_STATE, S5_GROUP), (2 * S5_GROUP) ** -0.5),
        "s5_c_re": nrm(16, (ne, 2, S5_GROUPS, S5_GROUP, S5_STATE), (2 * S5_STATE) ** -0.5),
        "s5_c_im": nrm(17, (ne, 2, S5_GROUPS, S5_GROUP, S5_STATE), (2 * S5_STATE) ** -0.5),
        "s5_d": nrm(18, (ne, S5_WIDTH), 1.0),
        "s5_glu_w": nrm(19, (ne, S5_WIDTH, S5_WIDTH), S5_WIDTH ** -0.5),
        "s5_glu_b": nrm(20, (ne, S5_WIDTH), 0.02),
        "od_w_in": nrm(21, (no, D_MODEL, ODD_IN), D_MODEL ** -0.5) * odd_col,
        "od_w_out": nrm(22, (no, GQA_WIDTH, D_MODEL), DN_BETA * GQA_WIDTH ** -0.5),
        "q_norm_g": 1.0 + nrm(23, (no, HEAD_DIM), 0.02),
        "k_norm_g": 1.0 + nrm(24, (no, HEAD_DIM), 0.02),
    }


def reference(x, c, ctx, c_ctx, ada_w, ada_b, ln_g, ln_b, ev_w_in, ev_w_out, na_rpb,
              s5_a_re, s5_a_im, s5_log_dt, s5_b_re, s5_b_im, s5_c_re, s5_c_im, s5_d,
              s5_glu_w, s5_glu_b, od_w_in, od_w_out, q_norm_g, k_norm_g):
    xc = ctx
    for l in range(DEPTH):
        need_ctx = l < DEPTH - 1
        sh, sc, gt = ada_mod(c, ada_w[l], ada_b[l])
        shc, scc, gtc = ada_mod(c_ctx, ada_w[l], ada_b[l])
        h = x * (1.0 + sc[:, None, :]) + sh[:, None, :]
        hc = xc * (1.0 + scc) + shc
        if l % 2 == 0:
            i = l // 2
            y, yc = even_layer(h, hc, ev_w_in[i], ev_w_out[i], na_rpb[i],
                               s5_a_re[i], s5_a_im[i], s5_log_dt[i], s5_b_re[i], s5_b_im[i],
                               s5_c_re[i], s5_c_im[i], s5_d[i], s5_glu_w[i], s5_glu_b[i], need_ctx)
        else:
            i = l // 2
            y, yc = odd_layer(h, hc, od_w_in[i], od_w_out[i], q_norm_g[i], k_norm_g[i], need_ctx)
        x = layer_norm(DN_ALPHA * x + gt[:, None, :] * y, ln_g[l], ln_b[l])
        if need_ctx:
            xc = layer_norm(DN_ALPHA * xc + gtc * yc, ln_g[l], ln_b[l])
    return x
```

```python
import functools
import math

import numpy as np
import jax
import jax.numpy as jnp
from jax import lax
from jax.experimental import pallas as pl
from jax.experimental.pallas import tpu as pltpu

F32 = jnp.float32
BF16 = jnp.bfloat16

D_MODEL = 2048
SEQ = 8192
CTX = 256
NTOK = SEQ + CTX
DEPTH = 4
GRID_W = 64
ROWS = SEQ // GRID_W
HEAD_DIM = 128
NA_HEADS = 8
NA_WIDTH = NA_HEADS * HEAD_DIM
NA_KH = 8
NA_KW = 16
S5_WIDTH = 1024
S5_GROUP = 16
S5_GROUPS = S5_WIDTH // S5_GROUP
S5_STATE = 64
GQA_HEADS = 16
GQA_KV_HEADS = 4
GQA_WIDTH = GQA_HEADS * HEAD_DIM
GQA_KV_WIDTH = GQA_KV_HEADS * HEAD_DIM
ROPE_THETA = 10000.0
EVEN_IN = 4 * NA_WIDTH + 2 * S5_WIDTH
ODD_IN = 2 * GQA_WIDTH + 2 * GQA_KV_WIDTH
DN_ALPHA = (2 * DEPTH) ** 0.25
LN_EPS = 1e-6
RMS_EPS = 1e-6
ATTN_SCALE = HEAD_DIM ** -0.5
MASK_VALUE = -1e30

LANES = 128
SUBLANES = 8
VMEM_LIMIT_BYTES = 56 * 1024 * 1024

PROJ_TM = 1056
PROJ_TN = 512
OUT_TM = 384
OUT_TM_LAST = 512
NA_ROWS_PER_BLOCK = 4
NA_TQ = NA_ROWS_PER_BLOCK * GRID_W
NA_KEY_ROWS = NA_ROWS_PER_BLOCK + NA_KH - 1
NA_TK = NA_KEY_ROWS * GRID_W
S5_T = 128
S5_SLABS = 4
S5_SLAB_CH = S5_WIDTH // S5_SLABS
S5_GS = 8
GQA_TQ = 256
GQA_TK = 768


def _cparams(*sem):
    return pltpu.CompilerParams(dimension_semantics=sem, vmem_limit_bytes=VMEM_LIMIT_BYTES)


def _dot(a, b):
    return jnp.dot(a, b, preferred_element_type=F32)


def _dot_nt(a, b):
    return lax.dot_general(a, b, (((1,), (1,)), ((), ())), preferred_element_type=F32)


def _silu(x):
    return x * jax.nn.sigmoid(x)


ADA_TN = 1024


def _ada_kernel(c_ref, w_ref, b_ref, o_ref):
    s0 = _silu(c_ref[0])
    s1 = _silu(c_ref[1])
    for ch in range(ADA_TN // LANES):
        sl = slice(ch * LANES, (ch + 1) * LANES)
        w = w_ref[0, :, sl]
        b = b_ref[0, :, sl]
        o_ref[0, 0:1, sl] = jnp.sum(w * s0, axis=0, keepdims=True) + b
        o_ref[0, 1:2, sl] = jnp.sum(w * s1, axis=0, keepdims=True) + b


def _ada_mod(c, c_ctx, ada_w, ada_b):
    cb = jnp.broadcast_to(jnp.stack([c[0], c_ctx])[:, :, None], (2, D_MODEL, LANES))
    n = 3 * D_MODEL
    return pl.pallas_call(
        _ada_kernel,
        grid=(DEPTH, n // ADA_TN),
        in_specs=[
            pl.BlockSpec((2, D_MODEL, LANES), lambda l, j: (0, 0, 0)),
            pl.BlockSpec((1, D_MODEL, ADA_TN), lambda l, j: (l, 0, j)),
            pl.BlockSpec((1, 1, ADA_TN), lambda l, j: (l, 0, j)),
        ],
        out_specs=pl.BlockSpec((1, 2, ADA_TN), lambda l, j: (l, 0, j)),
        out_shape=jax.ShapeDtypeStruct((DEPTH, 2, n), F32),
        compiler_params=_cparams("parallel", "parallel"),
        name="ada_mod",
    )(cb, ada_w, ada_b.reshape(DEPTH, 1, n))


def _row_is_latent(i, tm):
    rows = i * tm + lax.broadcasted_iota(jnp.int32, (tm, 1), 0)
    return rows < SEQ


def _modulate(x_ref, sc_ref, sh_ref, h_scr):
    lat = _row_is_latent(pl.program_id(0), PROJ_TM)
    sc = jnp.where(lat, sc_ref[0:1, :], sc_ref[1:2, :])
    sh = jnp.where(lat, sh_ref[0:1, :], sh_ref[1:2, :])
    h_scr[...] = (x_ref[...] * (1.0 + sc) + sh).astype(BF16)


def _proj_even_kernel(x_ref, sc_ref, sh_ref, w_ref, o_ref, h_scr):
    @pl.when(pl.program_id(1) == 0)
    def _():
        _modulate(x_ref, sc_ref, sh_ref, h_scr)

    o_ref[...] = _dot(h_scr[...], w_ref[...]).astype(o_ref.dtype)


GQA_Q_TILES = GQA_WIDTH // PROJ_TN
GQA_QK_TILES = (GQA_WIDTH + GQA_KV_WIDTH) // PROJ_TN


def _proj_odd_kernel(x_ref, sc_ref, sh_ref, w_ref, gain_ref, cos_ref, sin_ref, o_ref, h_scr):
    j = pl.program_id(1)

    @pl.when(j == 0)
    def _():
        _modulate(x_ref, sc_ref, sh_ref, h_scr)

    acc = _dot(h_scr[...], w_ref[...])

    @pl.when(j < GQA_QK_TILES)
    def _():
        gain = gain_ref[0]
        cos = cos_ref[...]
        sin = sin_ref[...]
        for hh in range(PROJ_TN // HEAD_DIM):
            sl = slice(hh * HEAD_DIM, (hh + 1) * HEAD_DIM)
            z = acc[:, sl]
            z = z * lax.rsqrt(jnp.mean(z * z, axis=-1, keepdims=True) + RMS_EPS) * gain
            z = z * cos + pltpu.roll(z, HEAD_DIM // 2, axis=1) * sin
            o_ref[:, sl] = z.astype(o_ref.dtype)

    @pl.when(j >= GQA_QK_TILES)
    def _():
        o_ref[...] = acc.astype(o_ref.dtype)


def _proj_common_specs():
    return [
        pl.BlockSpec((PROJ_TM, D_MODEL), lambda i, j: (i, 0)),
        pl.BlockSpec((2, D_MODEL), lambda i, j: (0, 0)),
        pl.BlockSpec((2, D_MODEL), lambda i, j: (0, 0)),
        pl.BlockSpec((D_MODEL, PROJ_TN), lambda i, j: (0, j)),
    ]


def _proj_even(xa, sc, sh, w_bf16):
    return pl.pallas_call(
        _proj_even_kernel,
        grid=(NTOK // PROJ_TM, EVEN_IN // PROJ_TN),
        in_specs=_proj_common_specs(),
        out_specs=pl.BlockSpec((PROJ_TM, PROJ_TN), lambda i, j: (i, j)),
        out_shape=jax.ShapeDtypeStruct((NTOK, EVEN_IN), BF16),
        scratch_shapes=[pltpu.VMEM((PROJ_TM, D_MODEL), BF16)],
        compiler_params=_cparams("parallel", "arbitrary"),
        name="proj_even",
    )(xa, sc, sh, w_bf16)


def _proj_odd(xa, sc, sh, w_bf16, gains, cos2, sin2):
    return pl.pallas_call(
        _proj_odd_kernel,
        grid=(NTOK // PROJ_TM, ODD_IN // PROJ_TN),
        in_specs=_proj_common_specs() + [
            pl.BlockSpec((1, 1, HEAD_DIM), lambda i, j: (jnp.minimum(j // GQA_Q_TILES, 1), 0, 0)),
            pl.BlockSpec((PROJ_TM, HEAD_DIM), lambda i, j: (i, 0)),
            pl.BlockSpec((PROJ_TM, HEAD_DIM), lambda i, j: (i, 0)),
        ],
        out_specs=pl.BlockSpec((PROJ_TM, PROJ_TN), lambda i, j: (i, j)),
        out_shape=jax.ShapeDtypeStruct((NTOK, ODD_IN), BF16),
        scratch_shapes=[pltpu.VMEM((PROJ_TM, D_MODEL), BF16)],
        compiler_params=_cparams("parallel", "arbitrary"),
        name="proj_odd",
    )(xa, sc, sh, w_bf16, gains, cos2, sin2)


def _rope_tables():
    t = np.arange(SEQ)
    half = HEAD_DIM // 2
    per_axis = half // 2
    inv = ROPE_THETA ** (-np.arange(per_axis, dtype=np.float64) / per_axis)
    ang = np.concatenate([(t // GRID_W)[:, None] * inv, (t % GRID_W)[:, None] * inv], -1)
    cos = np.concatenate([np.cos(ang), np.ones((CTX, half))], 0)
    sin = np.concatenate([np.sin(ang), np.zeros((CTX, half))], 0)
    cos2 = np.concatenate([cos, cos], -1)
    sin2 = np.concatenate([-sin, sin], -1)
    return jnp.asarray(cos2, F32), jnp.asarray(sin2, F32)


NA_QBLOCKS = ROWS // NA_ROWS_PER_BLOCK
NA_LAST_KEY_ROW0 = ROWS - NA_KEY_ROWS


def _na_bias_index():
    dr = np.zeros((3, NA_TQ, NA_TK), np.int32)
    dc = np.zeros((3, NA_TQ, NA_TK), np.int32)
    ok = np.zeros((3, NA_TQ, NA_TK), bool)
    r_start = np.clip(np.arange(ROWS) - NA_KH // 2, 0, ROWS - NA_KH)
    c_start = np.clip(np.arange(GRID_W) - NA_KW // 2, 0, GRID_W - NA_KW)
    for p, qb in enumerate((0, 1, NA_QBLOCKS - 1)):
        kr0 = int(np.clip(NA_ROWS_PER_BLOCK * qb - NA_KH // 2, 0, NA_LAST_KEY_ROW0))
        for i in range(NA_ROWS_PER_BLOCK):
            r = NA_ROWS_PER_BLOCK * qb + i
            for b in range(NA_KEY_ROWS):
                kr = kr0 + b
                row_ok = r_start[r] <= kr < r_start[r] + NA_KH
                for c in range(GRID_W):
                    cc = np.arange(GRID_W)
                    col_ok = (cc >= c_start[c]) & (cc < c_start[c] + NA_KW)
                    qi = i * GRID_W + c
                    ks = slice(b * GRID_W, (b + 1) * GRID_W)
                    ok[p, qi, ks] = row_ok & col_ok
                    dr[p, qi, ks] = np.clip(kr - r + NA_KH - 1, 0, 2 * NA_KH - 2)
                    dc[p, qi, ks] = np.clip(cc - c + NA_KW - 1, 0, 2 * NA_KW - 2)
    return dr, dc, ok


_NA_DR, _NA_DC, _NA_OK = _na_bias_index()


def _na_bias_table(rpb):
    g = rpb.astype(F32)[:, _NA_DR, _NA_DC]
    g = jnp.where(_NA_OK[None], g, MASK_VALUE)
    return g.transpose(1, 0, 2, 3)


def _na_kernel(q_ref, k_ref, v_ref, b_ref, o_ref):
    qb = pl.program_id(1)
    q = q_ref[...]
    kc = k_ref[pl.ds(SEQ, CTX), :]
    vc = v_ref[pl.ds(SEQ, CTX), :]
    s_ctx = _dot_nt(q, kc) * ATTN_SCALE
    m_ctx = jnp.max(s_ctx, axis=-1, keepdims=True)

    @pl.when(qb < NA_QBLOCKS)
    def _():
        kr0 = jnp.clip(NA_ROWS_PER_BLOCK * qb - NA_KH // 2, 0, NA_LAST_KEY_ROW0)
        start = pl.multiple_of(kr0 * GRID_W, GRID_W)
        kl = k_ref[pl.ds(start, NA_TK), :]
        vl = v_ref[pl.ds(start, NA_TK), :]
        s_loc = _dot_nt(q, kl) * ATTN_SCALE + b_ref[0, 0]
        m = jnp.maximum(jnp.max(s_loc, axis=-1, keepdims=True), m_ctx)
        p_loc = jnp.exp(s_loc - m)
        p_ctx = jnp.exp(s_ctx - m)
        den = jnp.sum(p_loc, axis=-1, keepdims=True) + jnp.sum(p_ctx, axis=-1, keepdims=True)
        o = _dot(p_loc.astype(BF16), vl) + _dot(p_ctx.astype(BF16), vc)
        o_ref[...] = (o / den).astype(o_ref.dtype)

    @pl.when(qb == NA_QBLOCKS)
    def _():
        p_ctx = jnp.exp(s_ctx - m_ctx)
        den = jnp.sum(p_ctx, axis=-1, keepdims=True)
        o_ref[...] = (_dot(p_ctx.astype(BF16), vc) / den).astype(o_ref.dtype)


def _na_attention(pe, bias):
    kcol = NA_WIDTH // HEAD_DIM
    vcol = 2 * NA_WIDTH // HEAD_DIM

    def pattern(qb):
        return jnp.where(qb == 0, 0, jnp.where(qb >= NA_QBLOCKS - 1, 2, 1))

    return pl.pallas_call(
        _na_kernel,
        grid=(NA_HEADS, NA_QBLOCKS + 1),
        in_specs=[
            pl.BlockSpec((NA_TQ, HEAD_DIM), lambda h, qb: (qb, h)),
            pl.BlockSpec((NTOK, HEAD_DIM), lambda h, qb: (0, kcol + h)),
            pl.BlockSpec((NTOK, HEAD_DIM), lambda h, qb: (0, vcol + h)),
            pl.BlockSpec((1, 1, NA_TQ, NA_TK), lambda h, qb: (pattern(qb), h, 0, 0)),
        ],
        out_specs=pl.BlockSpec((NA_TQ, HEAD_DIM), lambda h, qb: (qb, h)),
        out_shape=jax.ShapeDtypeStruct((NTOK, NA_WIDTH), BF16),
        compiler_params=_cparams("parallel", "arbitrary"),
        name="na_attention",
    )(pe, pe, pe, bias)


S5_NCHUNK = NTOK // S5_T
S5_LAT_CHUNKS = SEQ // S5_T
S5_DROWS = S5_T * S5_GS


def _s5_chunk_index(d, j):
    return jnp.where(d == 0, (j + S5_LAT_CHUNKS) % S5_NCHUNK, S5_NCHUNK - 1 - j)


def _s5_kernel(u_ref, b_ref, c_ref, a_ref, y_ref, xr_ref, xi_ref, yf_ref, st_ref):
    d = pl.program_id(0)

    @pl.when(pl.program_id(1) == 0)
    def _():
        st_ref[...] = jnp.zeros_like(st_ref)

    for s in range(S5_SLABS):
        x = _dot(u_ref[:, s * S5_SLAB_CH:(s + 1) * S5_SLAB_CH], b_ref[0, s])
        for gs in range(S5_GS):
            xr_ref[s, pl.ds(gs, S5_T, stride=S5_GS), :] = x[:, gs * LANES:(gs + 1) * LANES]
            xi_ref[s, pl.ds(gs, S5_T, stride=S5_GS), :] = (
                x[:, (S5_GS + gs) * LANES:(S5_GS + gs + 1) * LANES])

    ar = a_ref[0, 0]
    ai = a_ref[0, 1]

    def step(t, carry):
        hr, hi = carry
        tt = jnp.where(d == 0, t, S5_T - 1 - t)
        row = pl.multiple_of(tt * S5_GS, S5_GS)
        nr = ar * hr - ai * hi + xr_ref[:, pl.ds(row, S5_GS), :]
        ni = ar * hi + ai * hr + xi_ref[:, pl.ds(row, S5_GS), :]
        xr_ref[:, pl.ds(row, S5_GS), :] = nr
        xi_ref[:, pl.ds(row, S5_GS), :] = ni
        return nr, ni

    hr, hi = lax.fori_loop(0, S5_T, step, (st_ref[0], st_ref[1]), unroll=8)
    st_ref[0] = hr
    st_ref[1] = hi

    pairs_per_half = LANES // (2 * S5_GROUP)
    lane_pair = lax.broadcasted_iota(jnp.int32, (S5_T, LANES), 1) // (2 * S5_GROUP)
    for s in range(S5_SLABS):
        hcat = jnp.concatenate([xr_ref[s], xi_ref[s]], axis=1).astype(BF16)
        yfull = _dot(hcat, c_ref[0, s])
        for hf in range(2):
            yf_ref[hf] = yfull[:, hf * LANES:(hf + 1) * LANES]
        for hf in range(2):
            acc = jnp.zeros((S5_T, LANES), F32)
            for gp in range(pairs_per_half):
                gs = hf * pairs_per_half + gp
                acc = jnp.where(lane_pair == gp, yf_ref[hf, pl.ds(gs, S5_T, stride=S5_GS), :], acc)
            col = s * S5_SLAB_CH + hf * LANES
            y_ref[0, :, col:col + LANES] = acc


def _s5_tables(a_re, a_im, log_dt, b_re, b_im, c_re, c_im):
    a_re = a_re.astype(F32); a_im = a_im.astype(F32)
    dt = jnp.exp(log_dt.astype(F32))[..., None]
    mag = jnp.exp(a_re * dt)
    lb_re = mag * jnp.cos(a_im * dt)
    lb_im = mag * jnp.sin(a_im * dt)
    den = a_re * a_re + a_im * a_im
    n_re, n_im = lb_re - 1.0, lb_im
    f_re = (n_re * a_re + n_im * a_im) / den
    f_im = (n_im * a_re - n_re * a_im) / den
    bb_re = f_re[..., None] * b_re - f_im[..., None] * b_im
    bb_im = f_re[..., None] * b_im + f_im[..., None] * b_re

    def split(z):
        return z.reshape((2, S5_SLABS, S5_GS, 2) + z.shape[2:])

    eye_gs = jnp.eye(S5_GS, dtype=F32)
    eye_gl = jnp.eye(2, dtype=F32)

    def drive(bb):
        z = split(bb)
        z = jnp.einsum('dsglpc,gh,lm->dsglchmp', z, eye_gs, eye_gl)
        return z.reshape(2, S5_SLABS, S5_SLAB_CH, S5_GS * LANES)

    bmat = jnp.concatenate([drive(bb_re), drive(bb_im)], axis=-1).astype(BF16)

    def readout(cc):
        z = split(cc.astype(F32))
        z = jnp.einsum('dshmcp,lm->dslphmc', z, eye_gl)
        return z.reshape(2, S5_SLABS, LANES, S5_SLAB_CH)

    cmat = jnp.concatenate([readout(c_re), -readout(c_im)], axis=2).astype(BF16)

    def dense(z):
        return split(z).reshape(2, S5_SLABS, S5_GS, LANES)

    avec = jnp.stack([dense(lb_re), dense(lb_im)], axis=1)
    return bmat, cmat, avec


def _s5_scan(pe, bmat, cmat, avec):
    ucol = 4 * NA_WIDTH // S5_WIDTH
    return pl.pallas_call(
        _s5_kernel,
        grid=(2, S5_NCHUNK),
        in_specs=[
            pl.BlockSpec((S5_T, S5_WIDTH), lambda d, j: (_s5_chunk_index(d, j), ucol)),
            pl.BlockSpec((1, S5_SLABS, S5_SLAB_CH, 2 * S5_GS * LANES), lambda d, j: (d, 0, 0, 0)),
            pl.BlockSpec((1, S5_SLABS, 2 * LANES, S5_SLAB_CH), lambda d, j: (d, 0, 0, 0)),
            pl.BlockSpec((1, 2, S5_SLABS, S5_GS, LANES), lambda d, j: (d, 0, 0, 0, 0)),
        ],
        out_specs=pl.BlockSpec((1, S5_T, S5_WIDTH), lambda d, j: (d, _s5_chunk_index(d, j), 0)),
        out_shape=jax.ShapeDtypeStruct((2, NTOK, S5_WIDTH), F32),
        scratch_shapes=[
            pltpu.VMEM((S5_SLABS, S5_DROWS, LANES), F32),
            pltpu.VMEM((S5_SLABS, S5_DROWS, LANES), F32),
            pltpu.VMEM((2, S5_DROWS, LANES), F32),
            pltpu.VMEM((2, S5_SLABS, S5_GS, LANES), F32),
        ],
        compiler_params=_cparams("arbitrary", "arbitrary"),
        name="s5_scan",
    )(pe, bmat, cmat, avec)


def _residual_layer_norm(x, y, gt_ref, lng_ref, lnb_ref, lat):
    gt = jnp.where(lat, gt_ref[0:1, :], gt_ref[1:2, :])
    r = DN_ALPHA * x + gt * y
    mu = jnp.mean(r, axis=-1, keepdims=True)
    rc = r - mu
    var = jnp.mean(rc * rc, axis=-1, keepdims=True)
    return rc * lax.rsqrt(var + LN_EPS) * lng_ref[...] + lnb_ref[...]


def _out_even_kernel(tm, x_ref, ya_ref, ga_ref, u_ref, gb_ref, yf_ref, yb_ref, sd_ref,
                     gw_ref, gbias_ref, wo_ref, gt_ref, lng_ref, lnb_ref, o_ref):
    lat = _row_is_latent(pl.program_id(0), tm)
    za = ya_ref[...].astype(F32) * _silu(ga_ref[...].astype(F32))
    yb = u_ref[...].astype(F32) * sd_ref[...] + yf_ref[0] + yb_ref[0]
    zb = jax.nn.gelu(yb, approximate=True)
    zb = zb * jax.nn.sigmoid(_dot(zb.astype(BF16), gw_ref[...]) + gbias_ref[...])
    zb = zb * _silu(gb_ref[...].astype(F32))
    cat = jnp.concatenate([za.astype(BF16), zb.astype(BF16)], axis=1)
    y = _dot(cat, wo_ref[...])
    o_ref[...] = _residual_layer_norm(x_ref[...], y, gt_ref, lng_ref, lnb_ref, lat)


def _out_odd_kernel(tm, x_ref, o0_ref, o1_ref, g0_ref, g1_ref, wo_ref, gt_ref, lng_ref, lnb_ref, o_ref):
    lat = _row_is_latent(pl.program_id(0), tm)
    z0 = o0_ref[...].astype(F32) * _silu(g0_ref[...].astype(F32))
    z1 = o1_ref[...].astype(F32) * _silu(g1_ref[...].astype(F32))
    cat = jnp.concatenate([z0.astype(BF16), z1.astype(BF16)], axis=1)
    y = _dot(cat, wo_ref[...])
    o_ref[...] = _residual_layer_norm(x_ref[...], y, gt_ref, lng_ref, lnb_ref, lat)


def _const_spec(shape):
    nd = len(shape)
    return pl.BlockSpec(shape, lambda i: (0,) * nd, pipeline_mode=pl.Buffered(1))


def _out_even(xa, ya, pe, ys, s5_d, glu_w, glu_b, w_out, gt, ln_g, ln_b):
    tm = OUT_TM
    half = S5_WIDTH

    def col(c):
        return pl.BlockSpec((tm, half), lambda i: (i, c))

    return pl.pallas_call(
        functools.partial(_out_even_kernel, tm),
        grid=(NTOK // tm,),
        in_specs=[
            pl.BlockSpec((tm, D_MODEL), lambda i: (i, 0)),
            col(0),
            col(3), col(4), col(5),
            pl.BlockSpec((1, tm, half), lambda i: (0, i, 0)),
            pl.BlockSpec((1, tm, half), lambda i: (1, i, 0)),
            _const_spec((1, half)),
            _const_spec((half, half)),
            _const_spec((1, half)),
            _const_spec((D_MODEL, D_MODEL)),
            _const_spec((2, D_MODEL)),
            _const_spec((1, D_MODEL)),
            _const_spec((1, D_MODEL)),
        ],
        out_specs=pl.BlockSpec((tm, D_MODEL), lambda i: (i, 0)),
        out_shape=jax.ShapeDtypeStruct((NTOK, D_MODEL), F32),
        compiler_params=_cparams("parallel"),
        name="out_even",
    )(xa, ya, pe, pe, pe, ys, ys, s5_d.reshape(1, half), glu_w, glu_b.reshape(1, half),
      w_out, gt, ln_g.reshape(1, D_MODEL), ln_b.reshape(1, D_MODEL))


def _out_odd(xa, oa, po, w_out, gt, ln_g, ln_b, last):
    tm = OUT_TM_LAST if last else OUT_TM
    nrows = SEQ if last else NTOK
    half = GQA_WIDTH // 2
    gcol = (GQA_WIDTH + 2 * GQA_KV_WIDTH) // half

    return pl.pallas_call(
        functools.partial(_out_odd_kernel, tm),
        grid=(nrows // tm,),
        in_specs=[
            pl.BlockSpec((tm, D_MODEL), lambda i: (i, 0)),
            pl.BlockSpec((tm, half), lambda i: (i, 0)),
            pl.BlockSpec((tm, half), lambda i: (i, 1)),
            pl.BlockSpec((tm, half), lambda i: (i, gcol)),
            pl.BlockSpec((tm, half), lambda i: (i, gcol + 1)),
            _const_spec((D_MODEL, D_MODEL)),
            _const_spec((2, D_MODEL)),
            _const_spec((1, D_MODEL)),
            _const_spec((1, D_MODEL)),
        ],
        out_specs=pl.BlockSpec((tm, D_MODEL), lambda i: (i, 0)),
        out_shape=jax.ShapeDtypeStruct((nrows, D_MODEL), F32),
        compiler_params=_cparams("parallel"),
        name="out_odd",
    )(xa, oa, oa, po, po, w_out, gt, ln_g.reshape(1, D_MODEL), ln_b.reshape(1, D_MODEL))


GQA_GROUP = GQA_HEADS // GQA_KV_HEADS
GQA_QBLOCKS = SEQ // GQA_TQ
GQA_KCHUNKS = NTOK // GQA_TK
GQA_M = GQA_GROUP * GQA_TQ


def _gqa_kernel(q_ref, k_ref, v_ref, o_ref, m_ref, l_ref, acc_ref):
    qb = pl.program_id(1)
    q = jnp.concatenate([q_ref[:, g * HEAD_DIM:(g + 1) * HEAD_DIM] for g in range(GQA_GROUP)], axis=0)

    def init():
        m_ref[...] = jnp.full_like(m_ref, MASK_VALUE)
        l_ref[...] = jnp.zeros_like(l_ref)
        acc_ref[...] = jnp.zeros_like(acc_ref)

    def update(start, size):
        k = k_ref[pl.ds(start, size), :]
        v = v_ref[pl.ds(start, size), :]
        s = _dot_nt(q, k)
        m_prev = m_ref[...]
        m_next = jnp.maximum(m_prev, jnp.max(s, axis=-1, keepdims=True))
        alpha = jnp.exp(m_prev - m_next)
        p = jnp.exp(s - m_next[:, 0:1])
        l_ref[...] = alpha * l_ref[...] + jnp.sum(p, axis=-1, keepdims=True)
        acc_ref[...] = alpha * acc_ref[...] + _dot(p.astype(BF16), v)
        m_ref[...] = m_next

    def finish():
        o = acc_ref[...] / l_ref[...]
        for g in range(GQA_GROUP):
            o_ref[:, g * HEAD_DIM:(g + 1) * HEAD_DIM] = o[g * GQA_TQ:(g + 1) * GQA_TQ].astype(o_ref.dtype)

    @pl.when(qb < GQA_QBLOCKS)
    def _():
        init()

        def body(c, carry):
            update(pl.multiple_of(c * GQA_TK, GQA_TK), GQA_TK)
            return carry

        lax.fori_loop(0, GQA_KCHUNKS, body, 0)
        finish()

    @pl.when(qb == GQA_QBLOCKS)
    def _():
        init()
        update(SEQ, CTX)
        finish()


def _gqa_attention(po, need_ctx):
    kcol = GQA_WIDTH // HEAD_DIM
    vcol = (GQA_WIDTH + GQA_KV_WIDTH) // HEAD_DIM
    gw = GQA_GROUP * HEAD_DIM
    nqb = GQA_QBLOCKS + (1 if need_ctx else 0)
    return pl.pallas_call(
        _gqa_kernel,
        grid=(GQA_KV_HEADS, nqb),
        in_specs=[
            pl.BlockSpec((GQA_TQ, gw), lambda h, qb: (qb, h)),
            pl.BlockSpec((NTOK, HEAD_DIM), lambda h, qb: (0, kcol + h)),
            pl.BlockSpec((NTOK, HEAD_DIM), lambda h, qb: (0, vcol + h)),
        ],
        out_specs=pl.BlockSpec((GQA_TQ, gw), lambda h, qb: (qb, h)),
        out_shape=jax.ShapeDtypeStruct((nqb * GQA_TQ, GQA_WIDTH), BF16),
        scratch_shapes=[
            pltpu.VMEM((GQA_M, HEAD_DIM), F32),
            pltpu.VMEM((GQA_M, HEAD_DIM), F32),
            pltpu.VMEM((GQA_M, HEAD_DIM), F32),
        ],
        compiler_params=_cparams("parallel", "arbitrary"),
        name="gqa_attention",
    )(po, po, po)


def kernel(x, c, ctx, c_ctx, ada_w, ada_b, ln_g, ln_b, ev_w_in, ev_w_out, na_rpb, s5_a_re, s5_a_im,
           s5_log_dt, s5_b_re, s5_b_im, s5_c_re, s5_c_im, s5_d, s5_glu_w, s5_glu_b, od_w_in,
           od_w_out, q_norm_g, k_norm_g):
    assert x.shape == (1, SEQ, D_MODEL) and ctx.shape == (1, CTX, D_MODEL)
    xa = jnp.concatenate([x[0], ctx[0]], axis=0)
    mod = _ada_mod(c, c_ctx, ada_w, ada_b)
    cos2, sin2 = _rope_tables()

    for l in range(DEPTH):
        i = l // 2
        sh = mod[l, :, :D_MODEL]
        sc = mod[l, :, D_MODEL:2 * D_MODEL]
        gt = mod[l, :, 2 * D_MODEL:]
        if l % 2 == 0:
            pe = _proj_even(xa, sc, sh, ev_w_in[i].astype(BF16))
            ya = _na_attention(pe, _na_bias_table(na_rpb[i]))
            bmat, cmat, avec = _s5_tables(s5_a_re[i], s5_a_im[i], s5_log_dt[i], s5_b_re[i],
                                          s5_b_im[i], s5_c_re[i], s5_c_im[i])
            ys = _s5_scan(pe, bmat, cmat, avec)
            xa = _out_even(xa, ya, pe, ys, s5_d[i], s5_glu_w[i].astype(BF16), s5_glu_b[i],
                           ev_w_out[i].astype(BF16), gt, ln_g[l], ln_b[l])
        else:
            last = l == DEPTH - 1
            gains = jnp.stack([q_norm_g[i] * ATTN_SCALE, k_norm_g[i]]).reshape(2, 1, HEAD_DIM)
            po = _proj_odd(xa, sc, sh, od_w_in[i].astype(BF16), gains, cos2, sin2)
            oa = _gqa_attention(po, need_ctx=not last)
            xa = _out_odd(xa, oa, po, od_w_out[i].astype(BF16), gt, ln_g[l], ln_b[l], last)
    return xa.reshape(1, SEQ, D_MODEL)
```

```python
import functools
import math

import numpy as np
import jax
import jax.numpy as jnp
from jax import lax
from jax.experimental import pallas as pl
from jax.experimental.pallas import tpu as pltpu

F32 = jnp.float32
BF16 = jnp.bfloat16

D_MODEL = 2048
SEQ = 8192
CTX = 256
NTOK = SEQ + CTX
DEPTH = 4
GRID_W = 64
ROWS = SEQ // GRID_W
HEAD_DIM = 128
NA_HEADS = 8
NA_WIDTH = NA_HEADS * HEAD_DIM
NA_KH = 8
NA_KW = 16
S5_WIDTH = 1024
S5_GROUP = 16
S5_GROUPS = S5_WIDTH // S5_GROUP
S5_STATE = 64
GQA_HEADS = 16
GQA_KV_HEADS = 4
GQA_WIDTH = GQA_HEADS * HEAD_DIM
GQA_KV_WIDTH = GQA_KV_HEADS * HEAD_DIM
ROPE_THETA = 10000.0
EVEN_IN = 4 * NA_WIDTH + 2 * S5_WIDTH
ODD_IN = 2 * GQA_WIDTH + 2 * GQA_KV_WIDTH
DN_ALPHA = (2 * DEPTH) ** 0.25
LN_EPS = 1e-6
RMS_EPS = 1e-6
ATTN_SCALE = HEAD_DIM ** -0.5
MASK_VALUE = -1e30

LANES = 128
SUBLANES = 8
VMEM_LIMIT_BYTES = 56 * 1024 * 1024

PROJ_TM = 1056
PROJ_TN = 512
OUT_TM = 384
OUT_TM_LAST = 512
NA_ROWS_PER_BLOCK = 4
NA_TQ = NA_ROWS_PER_BLOCK * GRID_W
NA_KEY_ROWS = 12
NA_TK = NA_KEY_ROWS * GRID_W
S5_T = 128
S5_SLABS = 4
S5_SLAB_CH = S5_WIDTH // S5_SLABS
S5_GS = 8
GQA_TQ = 256
GQA_TK = 256


def _cparams(*sem):
    return pltpu.CompilerParams(dimension_semantics=sem, vmem_limit_bytes=VMEM_LIMIT_BYTES)


def _dot(a, b):
    return jnp.dot(a, b, preferred_element_type=F32)


def _dot_nt(a, b):
    return lax.dot_general(a, b, (((1,), (1,)), ((), ())), preferred_element_type=F32)


def _silu(x):
    return x * jax.nn.sigmoid(x)


ADA_TN = 1024


def _ada_kernel(c_ref, w_ref, b_ref, o_ref):
    s0 = _silu(c_ref[0])
    s1 = _silu(c_ref[1])
    for ch in range(ADA_TN // LANES):
        sl = slice(ch * LANES, (ch + 1) * LANES)
        w = w_ref[0, :, sl]
        b = b_ref[0, :, sl]
        o_ref[0, 0:1, sl] = jnp.sum(w * s0, axis=0, keepdims=True) + b
        o_ref[0, 1:2, sl] = jnp.sum(w * s1, axis=0, keepdims=True) + b


def _ada_mod(c, c_ctx, ada_w, ada_b):
    cb = jnp.broadcast_to(jnp.stack([c[0], c_ctx])[:, :, None], (2, D_MODEL, LANES))
    n = 3 * D_MODEL
    return pl.pallas_call(
        _ada_kernel,
        grid=(DEPTH, n // ADA_TN),
        in_specs=[
            pl.BlockSpec((2, D_MODEL, LANES), lambda l, j: (0, 0, 0)),
            pl.BlockSpec((1, D_MODEL, ADA_TN), lambda l, j: (l, 0, j)),
            pl.BlockSpec((1, 1, ADA_TN), lambda l, j: (l, 0, j)),
        ],
        out_specs=pl.BlockSpec((1, 2, ADA_TN), lambda l, j: (l, 0, j)),
        out_shape=jax.ShapeDtypeStruct((DEPTH, 2, n), F32),
        compiler_params=_cparams("parallel", "parallel"),
        name="ada_mod",
    )(cb, ada_w, ada_b.reshape(DEPTH, 1, n))


def _row_is_latent(i, tm):
    rows = i * tm + lax.broadcasted_iota(jnp.int32, (tm, 1), 0)
    return rows < SEQ


def _modulate(x_ref, sc_ref, sh_ref, h_scr):
    lat = _row_is_latent(pl.program_id(0), PROJ_TM)
    sc = jnp.where(lat, sc_ref[0:1, :], sc_ref[1:2, :])
    sh = jnp.where(lat, sh_ref[0:1, :], sh_ref[1:2, :])
    h_scr[...] = (x_ref[...] * (1.0 + sc) + sh).astype(BF16)


def _proj_even_kernel(x_ref, sc_ref, sh_ref, w_ref, o_ref, h_scr):
    @pl.when(pl.program_id(1) == 0)
    def _():
        _modulate(x_ref, sc_ref, sh_ref, h_scr)

    o_ref[...] = _dot(h_scr[...], w_ref[...]).astype(o_ref.dtype)


GQA_Q_TILES = GQA_WIDTH // PROJ_TN
GQA_QK_TILES = (GQA_WIDTH + GQA_KV_WIDTH) // PROJ_TN


def _proj_odd_kernel(x_ref, sc_ref, sh_ref, w_ref, gain_ref, cos_ref, sin_ref, o_ref, h_scr):
    j = pl.program_id(1)

    @pl.when(j == 0)
    def _():
        _modulate(x_ref, sc_ref, sh_ref, h_scr)

    acc = _dot(h_scr[...], w_ref[...])

    @pl.when(j < GQA_QK_TILES)
    def _():
        gain = gain_ref[0]
        cos = cos_ref[...]
        sin = sin_ref[...]
        for hh in range(PROJ_TN // HEAD_DIM):
            sl = slice(hh * HEAD_DIM, (hh + 1) * HEAD_DIM)
            z = acc[:, sl]
            z = z * lax.rsqrt(jnp.mean(z * z, axis=-1, keepdims=True) + RMS_EPS) * gain
            z = z * cos + pltpu.roll(z, HEAD_DIM // 2, axis=1) * sin
            o_ref[:, sl] = z.astype(o_ref.dtype)

    @pl.when(j >= GQA_QK_TILES)
    def _():
        o_ref[...] = acc.astype(o_ref.dtype)


def _proj_common_specs():
    return [
        pl.BlockSpec((PROJ_TM, D_MODEL), lambda i, j: (i, 0)),
        pl.BlockSpec((2, D_MODEL), lambda i, j: (0, 0)),
        pl.BlockSpec((2, D_MODEL), lambda i, j: (0, 0)),
        pl.BlockSpec((D_MODEL, PROJ_TN), lambda i, j: (0, j)),
    ]


def _proj_even(xa, sc, sh, w_bf16):
    return pl.pallas_call(
        _proj_even_kernel,
        grid=(NTOK // PROJ_TM, EVEN_IN // PROJ_TN),
        in_specs=_proj_common_specs(),
        out_specs=pl.BlockSpec((PROJ_TM, PROJ_TN), lambda i, j: (i, j)),
        out_shape=jax.ShapeDtypeStruct((NTOK, EVEN_IN), BF16),
        scratch_shapes=[pltpu.VMEM((PROJ_TM, D_MODEL), BF16)],
        compiler_params=_cparams("parallel", "arbitrary"),
        name="proj_even",
    )(xa, sc, sh, w_bf16)


def _proj_odd(xa, sc, sh, w_bf16, gains, cos2, sin2):
    return pl.pallas_call(
        _proj_odd_kernel,
        grid=(NTOK // PROJ_TM, ODD_IN // PROJ_TN),
        in_specs=_proj_common_specs() + [
            pl.BlockSpec((1, 1, HEAD_DIM), lambda i, j: (jnp.minimum(j // GQA_Q_TILES, 1), 0, 0)),
            pl.BlockSpec((PROJ_TM, HEAD_DIM), lambda i, j: (i, 0)),
            pl.BlockSpec((PROJ_TM, HEAD_DIM), lambda i, j: (i, 0)),
        ],
        out_specs=pl.BlockSpec((PROJ_TM, PROJ_TN), lambda i, j: (i, j)),
        out_shape=jax.ShapeDtypeStruct((NTOK, ODD_IN), BF16),
        scratch_shapes=[pltpu.VMEM((PROJ_TM, D_MODEL), BF16)],
        compiler_params=_cparams("parallel", "arbitrary"),
        name="proj_odd",
    )(xa, sc, sh, w_bf16, gains, cos2, sin2)


def _rope_tables():
    t = np.arange(SEQ)
    half = HEAD_DIM // 2
    per_axis = half // 2
    inv = ROPE_THETA ** (-np.arange(per_axis, dtype=np.float64) / per_axis)
    ang = np.concatenate([(t // GRID_W)[:, None] * inv, (t % GRID_W)[:, None] * inv], -1)
    cos = np.concatenate([np.cos(ang), np.ones((CTX, half))], 0)
    sin = np.concatenate([np.sin(ang), np.zeros((CTX, half))], 0)
    cos2 = np.concatenate([cos, cos], -1)
    sin2 = np.concatenate([-sin, sin], -1)
    return jnp.asarray(cos2, F32), jnp.asarray(sin2, F32)


NA_QBLOCKS = ROWS // NA_ROWS_PER_BLOCK
NA_LAST_KEY_ROW0 = ROWS - NA_KEY_ROWS
NA_DR = 2 * NA_KH - 1
NA_KEY_PAIRS = NA_KEY_ROWS // 2
NA_TAB_BOTH = 0
NA_TAB_FIRST = NA_TAB_BOTH + NA_DR - 1
NA_TAB_SECOND = NA_TAB_FIRST + NA_DR
NA_TAB_NONE = NA_TAB_SECOND + NA_DR
NA_TAB = NA_TAB_NONE + 1


def _na_col_onehot():
    c = np.arange(GRID_W)
    c_start = np.clip(c - NA_KW // 2, 0, GRID_W - NA_KW)
    inside = (c[None, :] >= c_start[:, None]) & (c[None, :] < c_start[:, None] + NA_KW)
    dc = c[None, :] - c[:, None] + NA_KW - 1
    onehot = (dc[None] == np.arange(2 * NA_KW - 1)[:, None, None]) & inside[None]
    return onehot.astype(np.float32), inside


_NA_ONEHOT, _NA_INSIDE = _na_col_onehot()


def _na_bias_table(rpb):
    t = jnp.einsum('hak,kcz->hacz', rpb.astype(F32), _NA_ONEHOT, precision=lax.Precision.HIGHEST)
    t = jnp.where(_NA_INSIDE, t, MASK_VALUE)
    masked = jnp.full_like(t, MASK_VALUE)
    both = jnp.concatenate([t[:, :-1], t[:, 1:]], axis=-1)
    first = jnp.concatenate([t, masked], axis=-1)
    second = jnp.concatenate([masked, t], axis=-1)
    none = jnp.concatenate([masked[:, :1], masked[:, :1]], axis=-1)
    return jnp.concatenate([both, first, second, none], axis=1)


def _na_bias_panel(tab_ref, qb, kr0):
    rows = []
    for i in range(NA_ROWS_PER_BLOCK):
        r = NA_ROWS_PER_BLOCK * qb + i
        lo = jnp.clip(r - NA_KH // 2, 0, ROWS - NA_KH)
        tiles = []
        for j in range(NA_KEY_PAIRS):
            kr = kr0 + 2 * j
            in0 = (kr >= lo) & (kr < lo + NA_KH)
            in1 = (kr + 1 >= lo) & (kr + 1 < lo + NA_KH)
            dr = kr - r + NA_KH - 1
            idx = jnp.where(
                in0 & in1, NA_TAB_BOTH + jnp.clip(dr, 0, NA_DR - 2),
                jnp.where(in0, NA_TAB_FIRST + jnp.clip(dr, 0, NA_DR - 1),
                          jnp.where(in1, NA_TAB_SECOND + jnp.clip(dr + 1, 0, NA_DR - 1), NA_TAB_NONE)))
            tiles.append(tab_ref[0, idx])
        rows.append(jnp.concatenate(tiles, axis=1))
    return jnp.concatenate(rows, axis=0)


def _na_kernel(q_ref, k_ref, v_ref, tab_ref, o_ref):
    qb = pl.program_id(1)
    q = q_ref[...]
    kc = k_ref[pl.ds(SEQ, CTX), :]
    vc = v_ref[pl.ds(SEQ, CTX), :]
    s_ctx = _dot_nt(q, kc) * ATTN_SCALE
    m_ctx = jnp.max(s_ctx, axis=-1, keepdims=True)

    @pl.when(qb < NA_QBLOCKS)
    def _():
        kr0 = jnp.clip(NA_ROWS_PER_BLOCK * qb - NA_KH // 2, 0, NA_LAST_KEY_ROW0)
        start = pl.multiple_of(kr0 * GRID_W, GRID_W)
        kl = k_ref[pl.ds(start, NA_TK), :]
        vl = v_ref[pl.ds(start, NA_TK), :]
        s_loc = _dot_nt(q, kl) * ATTN_SCALE + _na_bias_panel(tab_ref, qb, kr0)
        m = jnp.maximum(jnp.max(s_loc, axis=-1, keepdims=True), m_ctx)
        p_loc = jnp.exp(s_loc - m)
        p_ctx = jnp.exp(s_ctx - m)
        den = jnp.sum(p_loc, axis=-1, keepdims=True) + jnp.sum(p_ctx, axis=-1, keepdims=True)
        o = _dot(p_loc.astype(BF16), vl) + _dot(p_ctx.astype(BF16), vc)
        o_ref[...] = (o / den).astype(o_ref.dtype)

    @pl.when(qb == NA_QBLOCKS)
    def _():
        p_ctx = jnp.exp(s_ctx - m_ctx)
        den = jnp.sum(p_ctx, axis=-1, keepdims=True)
        o_ref[...] = (_dot(p_ctx.astype(BF16), vc) / den).astype(o_ref.dtype)


def _na_attention(pe, table):
    kcol = NA_WIDTH // HEAD_DIM
    vcol = 2 * NA_WIDTH // HEAD_DIM
    return pl.pallas_call(
        _na_kernel,
        grid=(NA_HEADS, NA_QBLOCKS + 1),
        in_specs=[
            pl.BlockSpec((NA_TQ, HEAD_DIM), lambda h, qb: (qb, h)),
            pl.BlockSpec((NTOK, HEAD_DIM), lambda h, qb: (0, kcol + h)),
            pl.BlockSpec((NTOK, HEAD_DIM), lambda h, qb: (0, vcol + h)),
            pl.BlockSpec((1, NA_TAB, GRID_W, 2 * GRID_W), lambda h, qb: (h, 0, 0, 0)),
        ],
        out_specs=pl.BlockSpec((NA_TQ, HEAD_DIM), lambda h, qb: (qb, h)),
        out_shape=jax.ShapeDtypeStruct((NTOK, NA_WIDTH), BF16),
        compiler_params=_cparams("parallel", "arbitrary"),
        name="na_attention",
    )(pe, pe, pe, table)


S5_NCHUNK = NTOK // S5_T
S5_LAT_CHUNKS = SEQ // S5_T
S5_DROWS = S5_T * S5_GS


def _s5_chunk_index(d, j):
    return jnp.where(d == 0, (j + S5_LAT_CHUNKS) % S5_NCHUNK, S5_NCHUNK - 1 - j)


def _s5_kernel(u_ref, b_ref, c_ref, a_ref, y_ref, xr_ref, xi_ref, yf_ref, st_ref):
    d = pl.program_id(0)

    @pl.when(pl.program_id(1) == 0)
    def _():
        st_ref[...] = jnp.zeros_like(st_ref)

    for s in range(S5_SLABS):
        x = _dot(u_ref[:, s * S5_SLAB_CH:(s + 1) * S5_SLAB_CH], b_ref[0, s])
        for gs in range(S5_GS):
            xr_ref[s, pl.ds(gs, S5_T, stride=S5_GS), :] = x[:, gs * LANES:(gs + 1) * LANES]
            xi_ref[s, pl.ds(gs, S5_T, stride=S5_GS), :] = (
                x[:, (S5_GS + gs) * LANES:(S5_GS + gs + 1) * LANES])

    ar = a_ref[0, 0]
    ai = a_ref[0, 1]

    def step(t, carry):
        hr, hi = carry
        tt = jnp.where(d == 0, t, S5_T - 1 - t)
        row = pl.multiple_of(tt * S5_GS, S5_GS)
        nr = ar * hr - ai * hi + xr_ref[:, pl.ds(row, S5_GS), :]
        ni = ar * hi + ai * hr + xi_ref[:, pl.ds(row, S5_GS), :]
        xr_ref[:, pl.ds(row, S5_GS), :] = nr
        xi_ref[:, pl.ds(row, S5_GS), :] = ni
        return nr, ni

    hr, hi = lax.fori_loop(0, S5_T, step, (st_ref[0], st_ref[1]), unroll=8)
    st_ref[0] = hr
    st_ref[1] = hi

    pairs_per_half = LANES // (2 * S5_GROUP)
    lane_pair = lax.broadcasted_iota(jnp.int32, (S5_T, LANES), 1) // (2 * S5_GROUP)
    for s in range(S5_SLABS):
        hcat = jnp.concatenate([xr_ref[s], xi_ref[s]], axis=1).astype(BF16)
        yfull = _dot(hcat, c_ref[0, s])
        for hf in range(2):
            yf_ref[hf] = yfull[:, hf * LANES:(hf + 1) * LANES]
        for hf in range(2):
            acc = jnp.zeros((S5_T, LANES), F32)
            for gp in range(pairs_per_half):
                gs = hf * pairs_per_half + gp
                acc = jnp.where(lane_pair == gp, yf_ref[hf, pl.ds(gs, S5_T, stride=S5_GS), :], acc)
            col = s * S5_SLAB_CH + hf * LANES
            y_ref[0, :, col:col + LANES] = acc


def _s5_tables(a_re, a_im, log_dt, b_re, b_im, c_re, c_im):
    a_re = a_re.astype(F32); a_im = a_im.astype(F32)
    dt = jnp.exp(log_dt.astype(F32))[..., None]
    mag = jnp.exp(a_re * dt)
    lb_re = mag * jnp.cos(a_im * dt)
    lb_im = mag * jnp.sin(a_im * dt)
    den = a_re * a_re + a_im * a_im
    n_re, n_im = lb_re - 1.0, lb_im
    f_re = (n_re * a_re + n_im * a_im) / den
    f_im = (n_im * a_re - n_re * a_im) / den
    bb_re = f_re[..., None] * b_re - f_im[..., None] * b_im
    bb_im = f_re[..., None] * b_im + f_im[..., None] * b_re

    def split(z):
        return z.reshape((2, S5_SLABS, S5_GS, 2) + z.shape[2:])

    eye_gs = jnp.eye(S5_GS, dtype=F32)
    eye_gl = jnp.eye(2, dtype=F32)

    def drive(bb):
        z = split(bb)
        z = jnp.einsum('dsglpc,gh,lm->dsglchmp', z, eye_gs, eye_gl)
        return z.reshape(2, S5_SLABS, S5_SLAB_CH, S5_GS * LANES)

    bmat = jnp.concatenate([drive(bb_re), drive(bb_im)], axis=-1).astype(BF16)

    def readout(cc):
        z = split(cc.astype(F32))
        z = jnp.einsum('dshmcp,lm->dslphmc', z, eye_gl)
        return z.reshape(2, S5_SLABS, LANES, S5_SLAB_CH)

    cmat = jnp.concatenate([readout(c_re), -readout(c_im)], axis=2).astype(BF16)

    def dense(z):
        return split(z).reshape(2, S5_SLABS, S5_GS, LANES)

    avec = jnp.stack([dense(lb_re), dense(lb_im)], axis=1)
    return bmat, cmat, avec


def _s5_scan(pe, bmat, cmat, avec):
    ucol = 4 * NA_WIDTH // S5_WIDTH
    return pl.pallas_call(
        _s5_kernel,
        grid=(2, S5_NCHUNK),
        in_specs=[
            pl.BlockSpec((S5_T, S5_WIDTH), lambda d, j: (_s5_chunk_index(d, j), ucol)),
            pl.BlockSpec((1, S5_SLABS, S5_SLAB_CH, 2 * S5_GS * LANES), lambda d, j: (d, 0, 0, 0)),
            pl.BlockSpec((1, S5_SLABS, 2 * LANES, S5_SLAB_CH), lambda d, j: (d, 0, 0, 0)),
            pl.BlockSpec((1, 2, S5_SLABS, S5_GS, LANES), lambda d, j: (d, 0, 0, 0, 0)),
        ],
        out_specs=pl.BlockSpec((1, S5_T, S5_WIDTH), lambda d, j: (d, _s5_chunk_index(d, j), 0)),
        out_shape=jax.ShapeDtypeStruct((2, NTOK, S5_WIDTH), F32),
        scratch_shapes=[
            pltpu.VMEM((S5_SLABS, S5_DROWS, LANES), F32),
            pltpu.VMEM((S5_SLABS, S5_DROWS, LANES), F32),
            pltpu.VMEM((2, S5_DROWS, LANES), F32),
            pltpu.VMEM((2, S5_SLABS, S5_GS, LANES), F32),
        ],
        compiler_params=_cparams("arbitrary", "arbitrary"),
        name="s5_scan",
    )(pe, bmat, cmat, avec)


def _residual_layer_norm(x, y, gt_ref, lng_ref, lnb_ref, lat):
    gt = jnp.where(lat, gt_ref[0:1, :], gt_ref[1:2, :])
    r = DN_ALPHA * x + gt * y
    mu = jnp.mean(r, axis=-1, keepdims=True)
    rc = r - mu
    var = jnp.mean(rc * rc, axis=-1, keepdims=True)
    return rc * lax.rsqrt(var + LN_EPS) * lng_ref[...] + lnb_ref[...]


def _out_even_kernel(tm, x_ref, ya_ref, ga_ref, u_ref, gb_ref, yf_ref, yb_ref, sd_ref,
                     gw_ref, gbias_ref, wo_ref, gt_ref, lng_ref, lnb_ref, o_ref):
    lat = _row_is_latent(pl.program_id(0), tm)
    za = ya_ref[...].astype(F32) * _silu(ga_ref[...].astype(F32))
    yb = u_ref[...].astype(F32) * sd_ref[...] + yf_ref[0] + yb_ref[0]
    zb = jax.nn.gelu(yb, approximate=True)
    zb = zb * jax.nn.sigmoid(_dot(zb.astype(BF16), gw_ref[...]) + gbias_ref[...])
    zb = zb * _silu(gb_ref[...].astype(F32))
    cat = jnp.concatenate([za.astype(BF16), zb.astype(BF16)], axis=1)
    y = _dot(cat, wo_ref[...])
    o_ref[...] = _residual_layer_norm(x_ref[...], y, gt_ref, lng_ref, lnb_ref, lat)


def _out_odd_kernel(tm, x_ref, o0_ref, o1_ref, g0_ref, g1_ref, wo_ref, gt_ref, lng_ref, lnb_ref, o_ref):
    lat = _row_is_latent(pl.program_id(0), tm)
    z0 = o0_ref[...].astype(F32) * _silu(g0_ref[...].astype(F32))
    z1 = o1_ref[...].astype(F32) * _silu(g1_ref[...].astype(F32))
    cat = jnp.concatenate([z0.astype(BF16), z1.astype(BF16)], axis=1)
    y = _dot(cat, wo_ref[...])
    o_ref[...] = _residual_layer_norm(x_ref[...], y, gt_ref, lng_ref, lnb_ref, lat)


def _const_spec(shape):
    nd = len(shape)
    return pl.BlockSpec(shape, lambda i: (0,) * nd, pipeline_mode=pl.Buffered(1))


def _out_even(xa, ya, pe, ys, s5_d, glu_w, glu_b, w_out, gt, ln_g, ln_b):
    tm = OUT_TM
    half = S5_WIDTH

    def col(c):
        return pl.BlockSpec((tm, half), lambda i: (i, c))

    return pl.pallas_call(
        functools.partial(_out_even_kernel, tm),
        grid=(NTOK // tm,),
        in_specs=[
            pl.BlockSpec((tm, D_MODEL), lambda i: (i, 0)),
            col(0),
            col(3), col(4), col(5),
            pl.BlockSpec((1, tm, half), lambda i: (0, i, 0)),
            pl.BlockSpec((1, tm, half), lambda i: (1, i, 0)),
            _const_spec((1, half)),
            _const_spec((half, half)),
            _const_spec((1, half)),
            _const_spec((D_MODEL, D_MODEL)),
            _const_spec((2, D_MODEL)),
            _const_spec((1, D_MODEL)),
            _const_spec((1, D_MODEL)),
        ],
        out_specs=pl.BlockSpec((tm, D_MODEL), lambda i: (i, 0)),
        out_shape=jax.ShapeDtypeStruct((NTOK, D_MODEL), F32),
        compiler_params=_cparams("parallel"),
        name="out_even",
    )(xa, ya, pe, pe, pe, ys, ys, s5_d.reshape(1, half), glu_w, glu_b.reshape(1, half),
      w_out, gt, ln_g.reshape(1, D_MODEL), ln_b.reshape(1, D_MODEL))


def _out_odd(xa, oa, po, w_out, gt, ln_g, ln_b, last):
    tm = OUT_TM_LAST if last else OUT_TM
    nrows = SEQ if last else NTOK
    half = GQA_WIDTH // 2
    gcol = (GQA_WIDTH + 2 * GQA_KV_WIDTH) // half

    return pl.pallas_call(
        functools.partial(_out_odd_kernel, tm),
        grid=(nrows // tm,),
        in_specs=[
            pl.BlockSpec((tm, D_MODEL), lambda i: (i, 0)),
            pl.BlockSpec((tm, half), lambda i: (i, 0)),
            pl.BlockSpec((tm, half), lambda i: (i, 1)),
            pl.BlockSpec((tm, half), lambda i: (i, gcol)),
            pl.BlockSpec((tm, half), lambda i: (i, gcol + 1)),
            _const_spec((D_MODEL, D_MODEL)),
            _const_spec((2, D_MODEL)),
            _const_spec((1, D_MODEL)),
            _const_spec((1, D_MODEL)),
        ],
        out_specs=pl.BlockSpec((tm, D_MODEL), lambda i: (i, 0)),
        out_shape=jax.ShapeDtypeStruct((nrows, D_MODEL), F32),
        compiler_params=_cparams("parallel"),
        name="out_odd",
    )(xa, oa, oa, po, po, w_out, gt, ln_g.reshape(1, D_MODEL), ln_b.reshape(1, D_MODEL))


GQA_GROUP = GQA_HEADS // GQA_KV_HEADS
GQA_QBLOCKS = SEQ // GQA_TQ
GQA_KTILES = NTOK // GQA_TK
GQA_CTX_TILE = SEQ // GQA_TK
LOG2E = math.log2(math.e)
GQA_UNROLL = 4


def _gqa_kernel(q_ref, k_ref, vt_ref, o_ref, sa_ref, sb_ref, m_ref, l_ref, acc_ref):
    qb = pl.program_id(1)
    qs = [q_ref[:, g * HEAD_DIM:(g + 1) * HEAD_DIM] for g in range(GQA_GROUP)]
    is_ctx = qb == GQA_QBLOCKS

    def scores(tile, buf, g):
        k = k_ref[pl.ds(pl.multiple_of(tile * GQA_TK, GQA_TK), GQA_TK), :]
        s = _dot_nt(k, qs[g])
        buf[g] = s
        return jnp.max(s, axis=0, keepdims=True)

    def accumulate(tile, buf, g, mx):
        m_prev = m_ref[g]
        m_cur = jnp.maximum(m_prev, mx)
        alpha = jnp.exp2(m_prev - m_cur)
        p = jnp.exp2(buf[g] - m_cur)
        l_ref[g] = alpha * l_ref[g] + jnp.sum(p, axis=0, keepdims=True)
        acc_ref[g] = alpha * acc_ref[g] + _dot(vt_ref[0, tile], p.astype(BF16))
        m_ref[g] = m_cur

    def step(tile, cur, nxt, mxs):
        out = []
        for g in range(GQA_GROUP):
            out.append(scores(tile + 1, nxt, g))
            accumulate(tile, cur, g, mxs[g])
        return tuple(out)

    m_ref[...] = jnp.full_like(m_ref, MASK_VALUE)
    l_ref[...] = jnp.zeros_like(l_ref)
    acc_ref[...] = jnp.zeros_like(acc_ref)
    first = jnp.where(is_ctx, GQA_CTX_TILE, 0)
    mxs = tuple(scores(first, sa_ref, g) for g in range(GQA_GROUP))

    def group(j, mxs):
        for u in range(0, GQA_UNROLL, 2):
            mxs = step(GQA_UNROLL * j + u, sa_ref, sb_ref, mxs)
            mxs = step(GQA_UNROLL * j + u + 1, sb_ref, sa_ref, mxs)
        return mxs

    mxs = lax.fori_loop(0, jnp.where(is_ctx, 0, GQA_CTX_TILE // GQA_UNROLL), group, mxs)
    for g in range(GQA_GROUP):
        accumulate(GQA_CTX_TILE, sa_ref, g, mxs[g])
        o = acc_ref[g] / l_ref[g]
        o_ref[:, g * HEAD_DIM:(g + 1) * HEAD_DIM] = o.T.astype(o_ref.dtype)


def _gqa_attention(po, need_ctx):
    kcol = GQA_WIDTH // HEAD_DIM
    v0 = GQA_WIDTH + GQA_KV_WIDTH
    gw = GQA_GROUP * HEAD_DIM
    nqb = GQA_QBLOCKS + (1 if need_ctx else 0)
    vt = po[:, v0:v0 + GQA_KV_WIDTH].reshape(GQA_KTILES, GQA_TK, GQA_KV_HEADS, HEAD_DIM).transpose(2, 0, 3, 1)
    return pl.pallas_call(
        _gqa_kernel,
        grid=(GQA_KV_HEADS, nqb),
        in_specs=[
            pl.BlockSpec((GQA_TQ, gw), lambda h, qb: (qb, h)),
            pl.BlockSpec((NTOK, HEAD_DIM), lambda h, qb: (0, kcol + h)),
            pl.BlockSpec((1, GQA_KTILES, HEAD_DIM, GQA_TK), lambda h, qb: (h, 0, 0, 0)),
        ],
        out_specs=pl.BlockSpec((GQA_TQ, gw), lambda h, qb: (qb, h)),
        out_shape=jax.ShapeDtypeStruct((nqb * GQA_TQ, GQA_WIDTH), BF16),
        scratch_shapes=[
            pltpu.VMEM((GQA_GROUP, GQA_TK, GQA_TQ), F32),
            pltpu.VMEM((GQA_GROUP, GQA_TK, GQA_TQ), F32),
            pltpu.VMEM((GQA_GROUP, 1, GQA_TQ), F32),
            pltpu.VMEM((GQA_GROUP, 1, GQA_TQ), F32),
            pltpu.VMEM((GQA_GROUP, HEAD_DIM, GQA_TQ), F32),
        ],
        compiler_params=_cparams("parallel", "arbitrary"),
        name="gqa_attention",
    )(po, po, vt)


def kernel(x, c, ctx, c_ctx, ada_w, ada_b, ln_g, ln_b, ev_w_in, ev_w_out, na_rpb, s5_a_re, s5_a_im,
           s5_log_dt, s5_b_re, s5_b_im, s5_c_re, s5_c_im, s5_d, s5_glu_w, s5_glu_b, od_w_in,
           od_w_out, q_norm_g, k_norm_g):
    assert x.shape == (1, SEQ, D_MODEL) and ctx.shape == (1, CTX, D_MODEL)
    xa = jnp.concatenate([x[0], ctx[0]], axis=0)
    mod = _ada_mod(c, c_ctx, ada_w, ada_b)
    cos2, sin2 = _rope_tables()

    for l in range(DEPTH):
        i = l // 2
        sh = mod[l, :, :D_MODEL]
        sc = mod[l, :, D_MODEL:2 * D_MODEL]
        gt = mod[l, :, 2 * D_MODEL:]
        if l % 2 == 0:
            pe = _proj_even(xa, sc, sh, ev_w_in[i].astype(BF16))
            ya = _na_attention(pe, _na_bias_table(na_rpb[i]))
            bmat, cmat, avec = _s5_tables(s5_a_re[i], s5_a_im[i], s5_log_dt[i], s5_b_re[i],
                                          s5_b_im[i], s5_c_re[i], s5_c_im[i])
            ys = _s5_scan(pe, bmat, cmat, avec)
            xa = _out_even(xa, ya, pe, ys, s5_d[i], s5_glu_w[i].astype(BF16), s5_glu_b[i],
                           ev_w_out[i].astype(BF16), gt, ln_g[l], ln_b[l])
        else:
            last = l == DEPTH - 1
            gains = jnp.stack([q_norm_g[i] * (ATTN_SCALE * LOG2E), k_norm_g[i]]).reshape(2, 1, HEAD_DIM)
            po = _proj_odd(xa, sc, sh, od_w_in[i].astype(BF16), gains, cos2, sin2)
            oa = _gqa_attention(po, need_ctx=not last)
            xa = _out_odd(xa, oa, po, od_w_out[i].astype(BF16), gt, ln_g[l], ln_b[l], last)
    return xa.reshape(1, SEQ, D_MODEL)
```

```python
import functools
import math

import numpy as np
import jax
import jax.numpy as jnp
from jax import lax
from jax.experimental import pallas as pl
from jax.experimental.pallas import tpu as pltpu

F32 = jnp.float32
BF16 = jnp.bfloat16

D_MODEL = 2048
SEQ = 8192
CTX = 256
NTOK = SEQ + CTX
DEPTH = 4
GRID_W = 64
ROWS = SEQ // GRID_W
HEAD_DIM = 128
NA_HEADS = 8
NA_WIDTH = NA_HEADS * HEAD_DIM
NA_KH = 8
NA_KW = 16
S5_WIDTH = 1024
S5_GROUP = 16
S5_GROUPS = S5_WIDTH // S5_GROUP
S5_STATE = 64
GQA_HEADS = 16
GQA_KV_HEADS = 4
GQA_WIDTH = GQA_HEADS * HEAD_DIM
GQA_KV_WIDTH = GQA_KV_HEADS * HEAD_DIM
ROPE_THETA = 10000.0
EVEN_IN = 4 * NA_WIDTH + 2 * S5_WIDTH
ODD_IN = 2 * GQA_WIDTH + 2 * GQA_KV_WIDTH
DN_ALPHA = (2 * DEPTH) ** 0.25
LN_EPS = 1e-6
RMS_EPS = 1e-6
ATTN_SCALE = HEAD_DIM ** -0.5
LOG2E = math.log2(math.e)
MASK_VALUE = -1e30

LANES = 128
SUBLANES = 8
VMEM_LIMIT_BYTES = 56 * 1024 * 1024

PROJ_TM = 1056
PROJ_TN = 512
OUT_TM = 384
OUT_TM_LAST = 512
NA_ROWS_PER_BLOCK = 4
NA_TQ = NA_ROWS_PER_BLOCK * GRID_W
NA_KEY_ROWS = 12
NA_TK = NA_KEY_ROWS * GRID_W
S5_T = 128
S5_SLABS = 4
S5_SLAB_CH = S5_WIDTH // S5_SLABS
S5_GS = 8
GQA_TQ = 256
GQA_TK = 256


def _cparams(*sem):
    return pltpu.CompilerParams(dimension_semantics=sem, vmem_limit_bytes=VMEM_LIMIT_BYTES)


def _dot(a, b):
    return jnp.dot(a, b, preferred_element_type=F32)


def _silu(x):
    return x * jax.nn.sigmoid(x)


ADA_TN = 1024


def _ada_kernel(c_ref, w_ref, b_ref, o_ref):
    s0 = _silu(c_ref[0])
    s1 = _silu(c_ref[1])
    for ch in range(ADA_TN // LANES):
        sl = slice(ch * LANES, (ch + 1) * LANES)
        w = w_ref[0, :, sl]
        b = b_ref[0, :, sl]
        o_ref[0, 0:1, sl] = jnp.sum(w * s0, axis=0, keepdims=True) + b
        o_ref[0, 1:2, sl] = jnp.sum(w * s1, axis=0, keepdims=True) + b


def _ada_mod(c, c_ctx, ada_w, ada_b):
    cb = jnp.broadcast_to(jnp.stack([c[0], c_ctx])[:, :, None], (2, D_MODEL, LANES))
    n = 3 * D_MODEL
    return pl.pallas_call(
        _ada_kernel,
        grid=(DEPTH, n // ADA_TN),
        in_specs=[
            pl.BlockSpec((2, D_MODEL, LANES), lambda l, j: (0, 0, 0)),
            pl.BlockSpec((1, D_MODEL, ADA_TN), lambda l, j: (l, 0, j)),
            pl.BlockSpec((1, 1, ADA_TN), lambda l, j: (l, 0, j)),
        ],
        out_specs=pl.BlockSpec((1, 2, ADA_TN), lambda l, j: (l, 0, j)),
        out_shape=jax.ShapeDtypeStruct((DEPTH, 2, n), F32),
        compiler_params=_cparams("parallel", "parallel"),
        name="ada_mod",
    )(cb, ada_w, ada_b.reshape(DEPTH, 1, n))


def _row_is_latent(i, tm):
    rows = i * tm + lax.broadcasted_iota(jnp.int32, (tm, 1), 0)
    return rows < SEQ


def _modulate(x_ref, sc_ref, sh_ref, h_scr):
    lat = _row_is_latent(pl.program_id(0), PROJ_TM)
    sc = jnp.where(lat, sc_ref[0:1, :], sc_ref[1:2, :])
    sh = jnp.where(lat, sh_ref[0:1, :], sh_ref[1:2, :])
    h_scr[...] = (x_ref[...] * (1.0 + sc) + sh).astype(BF16)


def _proj_even_kernel(x_ref, sc_ref, sh_ref, w_ref, o_ref, h_scr):
    @pl.when(pl.program_id(1) == 0)
    def _():
        _modulate(x_ref, sc_ref, sh_ref, h_scr)

    qscale = jnp.where(pl.program_id(1) < NA_WIDTH // PROJ_TN, ATTN_SCALE * LOG2E, 1.0)
    o_ref[...] = (_dot(h_scr[...], w_ref[...]) * qscale).astype(o_ref.dtype)


GQA_Q_TILES = GQA_WIDTH // PROJ_TN
GQA_QK_TILES = (GQA_WIDTH + GQA_KV_WIDTH) // PROJ_TN


def _proj_odd_kernel(x_ref, sc_ref, sh_ref, w_ref, gain_ref, cos_ref, sin_ref, o_ref, h_scr):
    j = pl.program_id(1)

    @pl.when(j == 0)
    def _():
        _modulate(x_ref, sc_ref, sh_ref, h_scr)

    acc = _dot(h_scr[...], w_ref[...])

    @pl.when(j < GQA_QK_TILES)
    def _():
        gain = gain_ref[0]
        cos = cos_ref[...]
        sin = sin_ref[...]
        for hh in range(PROJ_TN // HEAD_DIM):
            sl = slice(hh * HEAD_DIM, (hh + 1) * HEAD_DIM)
            z = acc[:, sl]
            z = z * lax.rsqrt(jnp.mean(z * z, axis=-1, keepdims=True) + RMS_EPS) * gain
            z = z * cos + pltpu.roll(z, HEAD_DIM // 2, axis=1) * sin
            o_ref[:, sl] = z.astype(o_ref.dtype)

    @pl.when(j >= GQA_QK_TILES)
    def _():
        o_ref[...] = acc.astype(o_ref.dtype)


def _proj_common_specs(layer):
    return [
        pl.BlockSpec((PROJ_TM, D_MODEL), lambda i, j: (i, 0)),
        pl.BlockSpec((2, D_MODEL), lambda i, j: (0, 0)),
        pl.BlockSpec((2, D_MODEL), lambda i, j: (0, 0)),
        pl.BlockSpec((None, D_MODEL, PROJ_TN), lambda i, j: (layer, 0, j)),
    ]


def _proj_even(xa, sc, sh, w_bf16, layer):
    return pl.pallas_call(
        _proj_even_kernel,
        grid=(NTOK // PROJ_TM, EVEN_IN // PROJ_TN),
        in_specs=_proj_common_specs(layer),
        out_specs=pl.BlockSpec((PROJ_TM, PROJ_TN), lambda i, j: (i, j)),
        out_shape=jax.ShapeDtypeStruct((NTOK, EVEN_IN), BF16),
        scratch_shapes=[pltpu.VMEM((PROJ_TM, D_MODEL), BF16)],
        compiler_params=_cparams("parallel", "arbitrary"),
        name="proj_even",
    )(xa, sc, sh, w_bf16)


def _proj_odd(xa, sc, sh, w_bf16, layer, gains, cos2, sin2):
    return pl.pallas_call(
        _proj_odd_kernel,
        grid=(NTOK // PROJ_TM, ODD_IN // PROJ_TN),
        in_specs=_proj_common_specs(layer) + [
            pl.BlockSpec((1, 1, HEAD_DIM), lambda i, j: (jnp.minimum(j // GQA_Q_TILES, 1), 0, 0)),
            pl.BlockSpec((PROJ_TM, HEAD_DIM), lambda i, j: (i, 0)),
            pl.BlockSpec((PROJ_TM, HEAD_DIM), lambda i, j: (i, 0)),
        ],
        out_specs=pl.BlockSpec((PROJ_TM, PROJ_TN), lambda i, j: (i, j)),
        out_shape=jax.ShapeDtypeStruct((NTOK, ODD_IN), BF16),
        scratch_shapes=[pltpu.VMEM((PROJ_TM, D_MODEL), BF16)],
        compiler_params=_cparams("parallel", "arbitrary"),
        name="proj_odd",
    )(xa, sc, sh, w_bf16, gains, cos2, sin2)


def _rope_tables():
    t = np.arange(SEQ)
    half = HEAD_DIM // 2
    per_axis = half // 2
    inv = ROPE_THETA ** (-np.arange(per_axis, dtype=np.float64) / per_axis)
    ang = np.concatenate([(t // GRID_W)[:, None] * inv, (t % GRID_W)[:, None] * inv], -1)
    cos = np.concatenate([np.cos(ang), np.ones((CTX, half))], 0)
    sin = np.concatenate([np.sin(ang), np.zeros((CTX, half))], 0)
    cos2 = np.concatenate([cos, cos], -1)
    sin2 = np.concatenate([-sin, sin], -1)
    return jnp.asarray(cos2, F32), jnp.asarray(sin2, F32)


NA_QBLOCKS = ROWS // NA_ROWS_PER_BLOCK
NA_LAST_KEY_ROW0 = ROWS - NA_KEY_ROWS
NA_DR = 2 * NA_KH - 1
NA_QPAIRS = NA_ROWS_PER_BLOCK // 2
NA_VT_TILE = 256
NA_VT_TILES = NTOK // NA_VT_TILE
NA_SLAB_TILES = NA_TK // NA_VT_TILE
NA_CTX_TILE = SEQ // NA_VT_TILE
NA_ONES_ROWS = 16
NA_VROWS = HEAD_DIM + NA_ONES_ROWS
NA_TAB_BOTH = 0
NA_TAB_FIRST = NA_TAB_BOTH + NA_DR - 1
NA_TAB_SECOND = NA_TAB_FIRST + NA_DR
NA_TAB_NONE = NA_TAB_SECOND + NA_DR
NA_TAB = NA_TAB_NONE + 1


def _na_col_onehot():
    c = np.arange(GRID_W)
    c_start = np.clip(c - NA_KW // 2, 0, GRID_W - NA_KW)
    inside = (c[None, :] >= c_start[:, None]) & (c[None, :] < c_start[:, None] + NA_KW)
    dc = c[None, :] - c[:, None] + NA_KW - 1
    onehot = (dc[None] == np.arange(2 * NA_KW - 1)[:, None, None]) & inside[None]
    return onehot.astype(np.float32), inside


_NA_ONEHOT, _NA_INSIDE = _na_col_onehot()


def _na_bias_table(rpb):
    t = jnp.einsum('hak,kcz->hazc', rpb.astype(F32) * LOG2E, _NA_ONEHOT, precision=lax.Precision.HIGHEST)
    t = jnp.where(_NA_INSIDE.T, t, MASK_VALUE)
    masked = jnp.full_like(t, MASK_VALUE)
    both = jnp.concatenate([t[:, 1:], t[:, :-1]], axis=-1)
    first = jnp.concatenate([t, masked], axis=-1)
    second = jnp.concatenate([masked, t], axis=-1)
    none = jnp.concatenate([masked[:, :1], masked[:, :1]], axis=-1)
    return jnp.concatenate([both, first, second, none], axis=1)


def _na_tile_index():
    idx = np.zeros((NA_QBLOCKS, NA_KEY_ROWS, NA_QPAIRS), np.int32)
    for qb in range(NA_QBLOCKS):
        kr0 = int(np.clip(NA_ROWS_PER_BLOCK * qb - NA_KH // 2, 0, NA_LAST_KEY_ROW0))
        for b in range(NA_KEY_ROWS):
            kr = kr0 + b
            for ip in range(NA_QPAIRS):
                r = NA_ROWS_PER_BLOCK * qb + 2 * ip
                lo0 = int(np.clip(r - NA_KH // 2, 0, ROWS - NA_KH))
                lo1 = int(np.clip(r + 1 - NA_KH // 2, 0, ROWS - NA_KH))
                in0 = lo0 <= kr < lo0 + NA_KH
                in1 = lo1 <= kr < lo1 + NA_KH
                dr1 = kr - (r + 1) + NA_KH - 1
                if in0 and in1:
                    idx[qb, b, ip] = NA_TAB_BOTH + dr1
                elif in0:
                    idx[qb, b, ip] = NA_TAB_FIRST + dr1 + 1
                elif in1:
                    idx[qb, b, ip] = NA_TAB_SECOND + dr1
                else:
                    idx[qb, b, ip] = NA_TAB_NONE
    return idx.reshape(-1)


_NA_TILE_INDEX = _na_tile_index()


def _na_bias_panel(idx_ref, tab_ref, qb):
    base = qb * (NA_KEY_ROWS * NA_QPAIRS)
    rows = []
    for b in range(NA_KEY_ROWS):
        tiles = [tab_ref[0, idx_ref[base + b * NA_QPAIRS + ip]] for ip in range(NA_QPAIRS)]
        rows.append(jnp.concatenate(tiles, axis=1))
    return jnp.concatenate(rows, axis=0)


def _na_kernel(idx_ref, q_ref, k_ref, vt_ref, tab_ref, o_ref, sa_ref, sb_ref):
    def q_transposed(row0):
        return q_ref[pl.ds(row0, NA_TQ), :].astype(F32).T.astype(BF16)

    def key_row0(qb):
        return jnp.clip(NA_ROWS_PER_BLOCK * qb - NA_KH // 2, 0, NA_LAST_KEY_ROW0)

    def scores(qb, buf):
        qt = q_transposed(pl.multiple_of(qb * NA_TQ, NA_TQ))
        kr0 = key_row0(qb)
        kl = k_ref[pl.ds(pl.multiple_of(kr0 * GRID_W, NA_VT_TILE), NA_TK), :]
        s_loc = _dot(kl, qt) + _na_bias_panel(idx_ref, tab_ref, qb)
        s_ctx = _dot(k_ref[pl.ds(SEQ, CTX), :], qt)
        buf[0:NA_TK] = s_loc
        buf[NA_TK:NA_TK + CTX] = s_ctx
        return jnp.maximum(jnp.max(s_loc, axis=0, keepdims=True), jnp.max(s_ctx, axis=0, keepdims=True))

    def emit(o_ext, row0):
        o = o_ext[0:HEAD_DIM] / o_ext[HEAD_DIM:HEAD_DIM + 1]
        o_ref[pl.ds(row0, NA_TQ), :] = o.T.astype(o_ref.dtype)

    def attend(qb, buf, mx):
        tile0 = key_row0(qb) * GRID_W // NA_VT_TILE
        o_ext = _dot(vt_ref[0, NA_CTX_TILE], jnp.exp2(buf[NA_TK:NA_TK + CTX] - mx).astype(BF16))
        for j in range(NA_SLAB_TILES):
            p = jnp.exp2(buf[j * NA_VT_TILE:(j + 1) * NA_VT_TILE] - mx).astype(BF16)
            o_ext = o_ext + _dot(vt_ref[0, tile0 + j], p)
        emit(o_ext, pl.multiple_of(qb * NA_TQ, NA_TQ))

    def pair(j, mx):
        mx_b = scores(2 * j + 1, sb_ref)
        attend(2 * j, sa_ref, mx)
        mx_a = scores(jnp.minimum(2 * j + 2, NA_QBLOCKS - 1), sa_ref)
        attend(2 * j + 1, sb_ref, mx_b)
        return mx_a

    lax.fori_loop(0, NA_QBLOCKS // 2, pair, scores(0, sa_ref))

    s_ctx = _dot(k_ref[pl.ds(SEQ, CTX), :], q_transposed(SEQ))
    p_ctx = jnp.exp2(s_ctx - jnp.max(s_ctx, axis=0, keepdims=True)).astype(BF16)
    emit(_dot(vt_ref[0, NA_CTX_TILE], p_ctx), SEQ)


def _na_attention(pe, table):
    kcol = NA_WIDTH // HEAD_DIM
    vt = pe[:, 2 * NA_WIDTH:3 * NA_WIDTH].reshape(NA_VT_TILES, NA_VT_TILE, NA_HEADS, HEAD_DIM).transpose(2, 0, 3, 1)
    vt = jnp.concatenate([vt, jnp.ones((NA_HEADS, NA_VT_TILES, NA_ONES_ROWS, NA_VT_TILE), BF16)], axis=2)
    return pl.pallas_call(
        _na_kernel,
        grid=(NA_HEADS,),
        in_specs=[
            pl.BlockSpec(memory_space=pltpu.SMEM),
            pl.BlockSpec((NTOK, HEAD_DIM), lambda h: (0, h)),
            pl.BlockSpec((NTOK, HEAD_DIM), lambda h: (0, kcol + h)),
            pl.BlockSpec((1, NA_VT_TILES, NA_VROWS, NA_VT_TILE), lambda h: (h, 0, 0, 0)),
            pl.BlockSpec((1, NA_TAB, GRID_W, 2 * GRID_W), lambda h: (h, 0, 0, 0)),
        ],
        out_specs=pl.BlockSpec((NTOK, HEAD_DIM), lambda h: (0, h)),
        out_shape=jax.ShapeDtypeStruct((NTOK, NA_WIDTH), BF16),
        scratch_shapes=[
            pltpu.VMEM((NA_TK + CTX, NA_TQ), F32),
            pltpu.VMEM((NA_TK + CTX, NA_TQ), F32),
        ],
        compiler_params=_cparams("parallel"),
        name="na_attention",
    )(jnp.asarray(_NA_TILE_INDEX), pe, pe, vt, table)


S5_NCHUNK = NTOK // S5_T
S5_LAT_CHUNKS = SEQ // S5_T
S5_DROWS = S5_T * S5_GS
S5_PAIR_CH = 2 * S5_GROUP


def _s5_chunk_index(d, j):
    return jnp.where(d == 0, (j + S5_LAT_CHUNKS) % S5_NCHUNK, S5_NCHUNK - 1 - j)


def _s5_kernel(u_ref, bc_ref, c_ref, a_ref, y_ref, b_ref, xr_ref, xi_ref, yf_ref, st_ref):
    d = pl.program_id(0)

    @pl.when(pl.program_id(1) == 0)
    def _():
        st_ref[...] = jnp.zeros_like(st_ref)
        b_ref[...] = jnp.zeros_like(b_ref)
        for s in range(S5_SLABS):
            for gs in range(S5_GS):
                rows = slice(gs * S5_PAIR_CH, (gs + 1) * S5_PAIR_CH)
                b_ref[s, rows, gs * LANES:(gs + 1) * LANES] = bc_ref[0, s, rows, 0:LANES]
                b_ref[s, rows, (S5_GS + gs) * LANES:(S5_GS + gs + 1) * LANES] = bc_ref[0, s, rows, LANES:2 * LANES]

    for s in range(S5_SLABS):
        x = _dot(u_ref[:, s * S5_SLAB_CH:(s + 1) * S5_SLAB_CH], b_ref[s])
        for gs in range(S5_GS):
            xr_ref[s, pl.ds(gs, S5_T, stride=S5_GS), :] = x[:, gs * LANES:(gs + 1) * LANES]
            xi_ref[s, pl.ds(gs, S5_T, stride=S5_GS), :] = (
                x[:, (S5_GS + gs) * LANES:(S5_GS + gs + 1) * LANES])

    ar = a_ref[0, 0]
    ai = a_ref[0, 1]

    def step(t, carry):
        hr, hi = carry
        tt = jnp.where(d == 0, t, S5_T - 1 - t)
        row = pl.multiple_of(tt * S5_GS, S5_GS)
        nr = ar * hr - ai * hi + xr_ref[:, pl.ds(row, S5_GS), :]
        ni = ar * hi + ai * hr + xi_ref[:, pl.ds(row, S5_GS), :]
        xr_ref[:, pl.ds(row, S5_GS), :] = nr
        xi_ref[:, pl.ds(row, S5_GS), :] = ni
        return nr, ni

    hr, hi = lax.fori_loop(0, S5_T, step, (st_ref[0], st_ref[1]), unroll=8)
    st_ref[0] = hr
    st_ref[1] = hi

    pairs_per_half = LANES // (2 * S5_GROUP)
    lane_pair = lax.broadcasted_iota(jnp.int32, (S5_T, LANES), 1) // (2 * S5_GROUP)
    for s in range(S5_SLABS):
        hcat = jnp.concatenate([xr_ref[s], xi_ref[s]], axis=1).astype(BF16)
        yfull = _dot(hcat, c_ref[0, s])
        for hf in range(2):
            yf_ref[hf] = yfull[:, hf * LANES:(hf + 1) * LANES]
        for hf in range(2):
            acc = jnp.zeros((S5_T, LANES), F32)
            for gp in range(pairs_per_half):
                gs = hf * pairs_per_half + gp
                acc = jnp.where(lane_pair == gp, yf_ref[hf, pl.ds(gs, S5_T, stride=S5_GS), :], acc)
            col = s * S5_SLAB_CH + hf * LANES
            y_ref[0, :, col:col + LANES] = acc


def _s5_tables(a_re, a_im, log_dt, b_re, b_im, c_re, c_im):
    a_re = a_re.astype(F32); a_im = a_im.astype(F32)
    dt = jnp.exp(log_dt.astype(F32))[..., None]
    mag = jnp.exp(a_re * dt)
    lb_re = mag * jnp.cos(a_im * dt)
    lb_im = mag * jnp.sin(a_im * dt)
    den = a_re * a_re + a_im * a_im
    n_re, n_im = lb_re - 1.0, lb_im
    f_re = (n_re * a_re + n_im * a_im) / den
    f_im = (n_im * a_re - n_re * a_im) / den
    bb_re = f_re[..., None] * b_re - f_im[..., None] * b_im
    bb_im = f_re[..., None] * b_im + f_im[..., None] * b_re

    def split(z):
        return z.reshape((2, S5_SLABS, S5_GS, 2) + z.shape[2:])

    eye_gl = jnp.eye(2, dtype=F32)

    def drive(bb):
        z = split(bb)
        z = jnp.einsum('dsglpc,lm->dsglcmp', z, eye_gl)
        return z.reshape(2, S5_SLABS, S5_SLAB_CH, LANES)

    bmat = jnp.concatenate([drive(bb_re), drive(bb_im)], axis=-1).astype(BF16)

    def readout(cc):
        z = split(cc.astype(F32))
        z = jnp.einsum('dshmcp,lm->dslphmc', z, eye_gl)
        return z.reshape(2, S5_SLABS, LANES, S5_SLAB_CH)

    cmat = jnp.concatenate([readout(c_re), -readout(c_im)], axis=2).astype(BF16)

    def dense(z):
        return split(z).reshape(2, S5_SLABS, S5_GS, LANES)

    avec = jnp.stack([dense(lb_re), dense(lb_im)], axis=1)
    return bmat, cmat, avec


def _s5_scan(pe, bmat, cmat, avec):
    ucol = 4 * NA_WIDTH // S5_WIDTH
    return pl.pallas_call(
        _s5_kernel,
        grid=(2, S5_NCHUNK),
        in_specs=[
            pl.BlockSpec((S5_T, S5_WIDTH), lambda d, j: (_s5_chunk_index(d, j), ucol)),
            pl.BlockSpec((1, S5_SLABS, S5_SLAB_CH, 2 * LANES), lambda d, j: (d, 0, 0, 0)),
            pl.BlockSpec((1, S5_SLABS, 2 * LANES, S5_SLAB_CH), lambda d, j: (d, 0, 0, 0)),
            pl.BlockSpec((1, 2, S5_SLABS, S5_GS, LANES), lambda d, j: (d, 0, 0, 0, 0)),
        ],
        out_specs=pl.BlockSpec((1, S5_T, S5_WIDTH), lambda d, j: (d, _s5_chunk_index(d, j), 0)),
        out_shape=jax.ShapeDtypeStruct((2, NTOK, S5_WIDTH), F32),
        scratch_shapes=[
            pltpu.VMEM((S5_SLABS, S5_SLAB_CH, 2 * S5_GS * LANES), BF16),
            pltpu.VMEM((S5_SLABS, S5_DROWS, LANES), F32),
            pltpu.VMEM((S5_SLABS, S5_DROWS, LANES), F32),
            pltpu.VMEM((2, S5_DROWS, LANES), F32),
            pltpu.VMEM((2, S5_SLABS, S5_GS, LANES), F32),
        ],
        compiler_params=_cparams("arbitrary", "arbitrary"),
        name="s5_scan",
    )(pe, bmat, cmat, avec)


def _residual_layer_norm(x, y, gt_ref, lng_ref, lnb_ref, lat):
    gt = jnp.where(lat, gt_ref[0:1, :], gt_ref[1:2, :])
    r = DN_ALPHA * x + gt * y
    mu = jnp.mean(r, axis=-1, keepdims=True)
    rc = r - mu
    var = jnp.mean(rc * rc, axis=-1, keepdims=True)
    return rc * lax.rsqrt(var + LN_EPS) * lng_ref[...] + lnb_ref[...]


def _out_even_kernel(tm, x_ref, ya_ref, ga_ref, u_ref, gb_ref, yf_ref, yb_ref, sd_ref,
                     gw_ref, gbias_ref, wo_ref, gt_ref, lng_ref, lnb_ref, o_ref):
    lat = _row_is_latent(pl.program_id(0), tm)
    za = ya_ref[...].astype(F32) * _silu(ga_ref[...].astype(F32))
    yb = u_ref[...].astype(F32) * sd_ref[...] + yf_ref[0] + yb_ref[0]
    zb = jax.nn.gelu(yb, approximate=True)
    zb = zb * jax.nn.sigmoid(_dot(zb.astype(BF16), gw_ref[...]) + gbias_ref[...])
    zb = zb * _silu(gb_ref[...].astype(F32))
    cat = jnp.concatenate([za.astype(BF16), zb.astype(BF16)], axis=1)
    y = _dot(cat, wo_ref[...])
    o_ref[...] = _residual_layer_norm(x_ref[...], y, gt_ref, lng_ref, lnb_ref, lat)


def _out_odd_kernel(tm, x_ref, o0_ref, o1_ref, g0_ref, g1_ref, wo_ref, gt_ref, lng_ref, lnb_ref, o_ref):
    lat = _row_is_latent(pl.program_id(0), tm)
    z0 = o0_ref[...].astype(F32) * _silu(g0_ref[...].astype(F32))
    z1 = o1_ref[...].astype(F32) * _silu(g1_ref[...].astype(F32))
    cat = jnp.concatenate([z0.astype(BF16), z1.astype(BF16)], axis=1)
    y = _dot(cat, wo_ref[...])
    o_ref[...] = _residual_layer_norm(x_ref[...], y, gt_ref, lng_ref, lnb_ref, lat)


def _const_spec(shape, layer=None):
    nd = len(shape)
    if layer is None:
        return pl.BlockSpec(shape, lambda i: (0,) * nd, pipeline_mode=pl.Buffered(1))
    return pl.BlockSpec((None,) + shape, lambda i: (layer,) + (0,) * nd, pipeline_mode=pl.Buffered(1))


def _out_even(xa, ya, pe, ys, s5_d, glu_w, glu_b, w_out, layer, gt, ln_g, ln_b):
    tm = OUT_TM
    half = S5_WIDTH

    def col(c):
        return pl.BlockSpec((tm, half), lambda i: (i, c))

    return pl.pallas_call(
        functools.partial(_out_even_kernel, tm),
        grid=(NTOK // tm,),
        in_specs=[
            pl.BlockSpec((tm, D_MODEL), lambda i: (i, 0)),
            col(0),
            col(3), col(4), col(5),
            pl.BlockSpec((1, tm, half), lambda i: (0, i, 0)),
            pl.BlockSpec((1, tm, half), lambda i: (1, i, 0)),
            _const_spec((1, half)),
            _const_spec((half, half), layer),
            _const_spec((1, half)),
            _const_spec((D_MODEL, D_MODEL), layer),
            _const_spec((2, D_MODEL)),
            _const_spec((1, D_MODEL)),
            _const_spec((1, D_MODEL)),
        ],
        out_specs=pl.BlockSpec((tm, D_MODEL), lambda i: (i, 0)),
        out_shape=jax.ShapeDtypeStruct((NTOK, D_MODEL), F32),
        compiler_params=_cparams("parallel"),
        name="out_even",
    )(xa, ya, pe, pe, pe, ys, ys, s5_d.reshape(1, half), glu_w, glu_b.reshape(1, half),
      w_out, gt, ln_g.reshape(1, D_MODEL), ln_b.reshape(1, D_MODEL))


def _out_odd(xa, oa, po, w_out, layer, gt, ln_g, ln_b, last):
    tm = OUT_TM_LAST if last else OUT_TM
    nrows = SEQ if last else NTOK
    half = GQA_WIDTH // 2
    gcol = (GQA_WIDTH + 2 * GQA_KV_WIDTH) // half

    return pl.pallas_call(
        functools.partial(_out_odd_kernel, tm),
        grid=(nrows // tm,),
        in_specs=[
            pl.BlockSpec((tm, D_MODEL), lambda i: (i, 0)),
            pl.BlockSpec((tm, half), lambda i: (i, 0)),
            pl.BlockSpec((tm, half), lambda i: (i, 1)),
            pl.BlockSpec((tm, half), lambda i: (i, gcol)),
            pl.BlockSpec((tm, half), lambda i: (i, gcol + 1)),
            _const_spec((D_MODEL, D_MODEL), layer),
            _const_spec((2, D_MODEL)),
            _const_spec((1, D_MODEL)),
            _const_spec((1, D_MODEL)),
        ],
        out_specs=pl.BlockSpec((tm, D_MODEL), lambda i: (i, 0)),
        out_shape=jax.ShapeDtypeStruct((nrows, D_MODEL), F32),
        compiler_params=_cparams("parallel"),
        name="out_odd",
    )(xa, oa, oa, po, po, w_out, gt, ln_g.reshape(1, D_MODEL), ln_b.reshape(1, D_MODEL))


GQA_GROUP = GQA_HEADS // GQA_KV_HEADS
GQA_QBLOCKS = SEQ // GQA_TQ
GQA_KTILES = NTOK // GQA_TK
GQA_CTX_TILE = SEQ // GQA_TK
GQA_ONES_ROWS = 16
GQA_VROWS = HEAD_DIM + GQA_ONES_ROWS
GQA_UNROLL = 4


def _gqa_kernel(q_ref, k_ref, vt_ref, o_ref, sa_ref, sb_ref, m_ref, acc_ref):
    qb = pl.program_id(1)
    qts = [q_ref[:, g * HEAD_DIM:(g + 1) * HEAD_DIM].astype(F32).T.astype(BF16) for g in range(GQA_GROUP)]
    is_ctx = qb == GQA_QBLOCKS

    def scores(tile, buf, g):
        k = k_ref[pl.ds(pl.multiple_of(tile * GQA_TK, GQA_TK), GQA_TK), :]
        s = _dot(k, qts[g])
        buf[g] = s
        return jnp.max(s, axis=0, keepdims=True)

    def accumulate(tile, buf, g, mx):
        m_prev = m_ref[g]
        m_cur = jnp.maximum(m_prev, mx)
        alpha = jnp.exp2(m_prev - m_cur)
        p = jnp.exp2(buf[g] - m_cur)
        acc_ref[g] = alpha * acc_ref[g] + _dot(vt_ref[0, tile], p.astype(BF16))
        m_ref[g] = m_cur

    def step(tile, cur, nxt, mxs):
        out = []
        for g in range(GQA_GROUP):
            out.append(scores(tile + 1, nxt, g))
            accumulate(tile, cur, g, mxs[g])
        return tuple(out)

    m_ref[...] = jnp.full_like(m_ref, MASK_VALUE)
    acc_ref[...] = jnp.zeros_like(acc_ref)
    first = jnp.where(is_ctx, GQA_CTX_TILE, 0)
    mxs = tuple(scores(first, sa_ref, g) for g in range(GQA_GROUP))

    def group(j, mxs):
        for u in range(0, GQA_UNROLL, 2):
            mxs = step(GQA_UNROLL * j + u, sa_ref, sb_ref, mxs)
            mxs = step(GQA_UNROLL * j + u + 1, sb_ref, sa_ref, mxs)
        return mxs

    mxs = lax.fori_loop(0, jnp.where(is_ctx, 0, GQA_CTX_TILE // GQA_UNROLL), group, mxs)
    for g in range(GQA_GROUP):
        accumulate(GQA_CTX_TILE, sa_ref, g, mxs[g])
        o = acc_ref[g, 0:HEAD_DIM] / acc_ref[g, HEAD_DIM:HEAD_DIM + 1]
        o_ref[:, g * HEAD_DIM:(g + 1) * HEAD_DIM] = o.T.astype(o_ref.dtype)


def _gqa_attention(po, need_ctx):
    kcol = GQA_WIDTH // HEAD_DIM
    v0 = GQA_WIDTH + GQA_KV_WIDTH
    gw = GQA_GROUP * HEAD_DIM
    nqb = GQA_QBLOCKS + (1 if need_ctx else 0)
    vt = po[:, v0:v0 + GQA_KV_WIDTH].reshape(GQA_KTILES, GQA_TK, GQA_KV_HEADS, HEAD_DIM).transpose(2, 0, 3, 1)
    vt = jnp.concatenate([vt, jnp.ones((GQA_KV_HEADS, GQA_KTILES, GQA_ONES_ROWS, GQA_TK), BF16)], axis=2)
    return pl.pallas_call(
        _gqa_kernel,
        grid=(GQA_KV_HEADS, nqb),
        in_specs=[
            pl.BlockSpec((GQA_TQ, gw), lambda h, qb: (qb, h)),
            pl.BlockSpec((NTOK, HEAD_DIM), lambda h, qb: (0, kcol + h)),
            pl.BlockSpec((1, GQA_KTILES, GQA_VROWS, GQA_TK), lambda h, qb: (h, 0, 0, 0)),
        ],
        out_specs=pl.BlockSpec((GQA_TQ, gw), lambda h, qb: (qb, h)),
        out_shape=jax.ShapeDtypeStruct((nqb * GQA_TQ, GQA_WIDTH), BF16),
        scratch_shapes=[
            pltpu.VMEM((GQA_GROUP, GQA_TK, GQA_TQ), F32),
            pltpu.VMEM((GQA_GROUP, GQA_TK, GQA_TQ), F32),
            pltpu.VMEM((GQA_GROUP, 1, GQA_TQ), F32),
            pltpu.VMEM((GQA_GROUP, GQA_VROWS, GQA_TQ), F32),
        ],
        compiler_params=_cparams("parallel", "arbitrary"),
        name="gqa_attention",
    )(po, po, vt)


def kernel(x, c, ctx, c_ctx, ada_w, ada_b, ln_g, ln_b, ev_w_in, ev_w_out, na_rpb, s5_a_re, s5_a_im,
           s5_log_dt, s5_b_re, s5_b_im, s5_c_re, s5_c_im, s5_d, s5_glu_w, s5_glu_b, od_w_in,
           od_w_out, q_norm_g, k_norm_g):
    assert x.shape == (1, SEQ, D_MODEL) and ctx.shape == (1, CTX, D_MODEL)
    xa = jnp.concatenate([x[0], ctx[0]], axis=0)
    mod = _ada_mod(c, c_ctx, ada_w, ada_b)
    cos2, sin2 = _rope_tables()
    ev_w_in_b = ev_w_in.astype(BF16)
    ev_w_out_b = ev_w_out.astype(BF16)
    glu_w_b = s5_glu_w.astype(BF16)
    od_w_in_b = od_w_in.astype(BF16)
    od_w_out_b = od_w_out.astype(BF16)

    for l in range(DEPTH):
        i = l // 2
        sh = mod[l, :, :D_MODEL]
        sc = mod[l, :, D_MODEL:2 * D_MODEL]
        gt = mod[l, :, 2 * D_MODEL:]
        if l % 2 == 0:
            pe = _proj_even(xa, sc, sh, ev_w_in_b, i)
            ya = _na_attention(pe, _na_bias_table(na_rpb[i]))
            bmat, cmat, avec = _s5_tables(s5_a_re[i], s5_a_im[i], s5_log_dt[i], s5_b_re[i],
                                          s5_b_im[i], s5_c_re[i], s5_c_im[i])
            ys = _s5_scan(pe, bmat, cmat, avec)
            xa = _out_even(xa, ya, pe, ys, s5_d[i], glu_w_b, s5_glu_b[i], ev_w_out_b, i,
                           gt, ln_g[l], ln_b[l])
        else:
            last = l == DEPTH - 1
            gains = jnp.stack([q_norm_g[i] * (ATTN_SCALE * LOG2E), k_norm_g[i]]).reshape(2, 1, HEAD_DIM)
            po = _proj_odd(xa, sc, sh, od_w_in_b, i, gains, cos2, sin2)
            oa = _gqa_attention(po, need_ctx=not last)
            xa = _out_odd(xa, oa, po, od_w_out_b, i, gt, ln_g[l], ln_b[l], last)
    return xa.reshape(1, SEQ, D_MODEL)
```

```python
import functools
import math

import numpy as np
import jax
import jax.numpy as jnp
from jax import lax
from jax.experimental import pallas as pl
from jax.experimental.pallas import tpu as pltpu

F32 = jnp.float32
BF16 = jnp.bfloat16

D_MODEL = 2048
SEQ = 8192
CTX = 256
NTOK = SEQ + CTX
DEPTH = 4
GRID_W = 64
ROWS = SEQ // GRID_W
HEAD_DIM = 128
NA_HEADS = 8
NA_WIDTH = NA_HEADS * HEAD_DIM
NA_KH = 8
NA_KW = 16
S5_WIDTH = 1024
S5_GROUP = 16
S5_GROUPS = S5_WIDTH // S5_GROUP
S5_STATE = 64
GQA_HEADS = 16
GQA_KV_HEADS = 4
GQA_WIDTH = GQA_HEADS * HEAD_DIM
GQA_KV_WIDTH = GQA_KV_HEADS * HEAD_DIM
ROPE_THETA = 10000.0
EVEN_IN = 4 * NA_WIDTH + 2 * S5_WIDTH
ODD_IN = 2 * GQA_WIDTH + 2 * GQA_KV_WIDTH
DN_ALPHA = (2 * DEPTH) ** 0.25
LN_EPS = 1e-6
RMS_EPS = 1e-6
ATTN_SCALE = HEAD_DIM ** -0.5
LOG2E = math.log2(math.e)
MASK_VALUE = -1e30

LANES = 128
SUBLANES = 8
VMEM_LIMIT_BYTES = 56 * 1024 * 1024

PROJ_TM = 1056
PROJ_TN = 512
OUT_TM = 384
OUT_TM_LAST = 512
NA_ROWS_PER_BLOCK = 4
NA_TQ = NA_ROWS_PER_BLOCK * GRID_W
NA_KEY_ROWS = 12
NA_TK = NA_KEY_ROWS * GRID_W
S5_T = 128
S5_SLABS = 4
S5_SLAB_CH = S5_WIDTH // S5_SLABS
S5_GS = 8
GQA_TQ = 256
GQA_TK = 256


def _cparams(*sem):
    return pltpu.CompilerParams(dimension_semantics=sem, vmem_limit_bytes=VMEM_LIMIT_BYTES)


def _dot(a, b):
    return jnp.dot(a, b, preferred_element_type=F32)


def _silu(x):
    return x * jax.nn.sigmoid(x)


ADA_TN = 1024


def _ada_kernel(c_ref, w_ref, b_ref, o_ref):
    s0 = _silu(c_ref[0])
    s1 = _silu(c_ref[1])
    for ch in range(ADA_TN // LANES):
        sl = slice(ch * LANES, (ch + 1) * LANES)
        w = w_ref[0, :, sl]
        b = b_ref[0, :, sl]
        o_ref[0, 0:1, sl] = jnp.sum(w * s0, axis=0, keepdims=True) + b
        o_ref[0, 1:2, sl] = jnp.sum(w * s1, axis=0, keepdims=True) + b


def _ada_mod(c, c_ctx, ada_w, ada_b):
    cb = jnp.broadcast_to(jnp.stack([c[0], c_ctx])[:, :, None], (2, D_MODEL, LANES))
    n = 3 * D_MODEL
    return pl.pallas_call(
        _ada_kernel,
        grid=(DEPTH, n // ADA_TN),
        in_specs=[
            pl.BlockSpec((2, D_MODEL, LANES), lambda l, j: (0, 0, 0)),
            pl.BlockSpec((1, D_MODEL, ADA_TN), lambda l, j: (l, 0, j)),
            pl.BlockSpec((1, 1, ADA_TN), lambda l, j: (l, 0, j)),
        ],
        out_specs=pl.BlockSpec((1, 2, ADA_TN), lambda l, j: (l, 0, j)),
        out_shape=jax.ShapeDtypeStruct((DEPTH, 2, n), F32),
        compiler_params=_cparams("parallel", "parallel"),
        name="ada_mod",
    )(cb, ada_w, ada_b.reshape(DEPTH, 1, n))


def _row_is_latent(i, tm):
    rows = i * tm + lax.broadcasted_iota(jnp.int32, (tm, 1), 0)
    return rows < SEQ


def _modulate(x_ref, sc_ref, sh_ref, h_scr):
    lat = _row_is_latent(pl.program_id(0), PROJ_TM)
    sc = jnp.where(lat, sc_ref[0:1, :], sc_ref[1:2, :])
    sh = jnp.where(lat, sh_ref[0:1, :], sh_ref[1:2, :])
    h_scr[...] = (x_ref[...] * (1.0 + sc) + sh).astype(BF16)


def _proj_even_kernel(x_ref, sc_ref, sh_ref, w_ref, o_ref, h_scr):
    @pl.when(pl.program_id(1) == 0)
    def _():
        _modulate(x_ref, sc_ref, sh_ref, h_scr)

    qscale = jnp.where(pl.program_id(1) < NA_WIDTH // PROJ_TN, ATTN_SCALE * LOG2E, 1.0)
    o_ref[...] = (_dot(h_scr[...], w_ref[...]) * qscale).astype(o_ref.dtype)


GQA_Q_TILES = GQA_WIDTH // PROJ_TN
GQA_QK_TILES = (GQA_WIDTH + GQA_KV_WIDTH) // PROJ_TN


def _proj_odd_kernel(x_ref, sc_ref, sh_ref, w_ref, gain_ref, cos_ref, sin_ref, o_ref, h_scr):
    j = pl.program_id(1)

    @pl.when(j == 0)
    def _():
        _modulate(x_ref, sc_ref, sh_ref, h_scr)

    acc = _dot(h_scr[...], w_ref[...])

    @pl.when(j < GQA_QK_TILES)
    def _():
        gain = gain_ref[0]
        cos = cos_ref[...]
        sin = sin_ref[...]
        for hh in range(PROJ_TN // HEAD_DIM):
            sl = slice(hh * HEAD_DIM, (hh + 1) * HEAD_DIM)
            z = acc[:, sl]
            z = z * lax.rsqrt(jnp.mean(z * z, axis=-1, keepdims=True) + RMS_EPS) * gain
            z = z * cos + pltpu.roll(z, HEAD_DIM // 2, axis=1) * sin
            o_ref[:, sl] = z.astype(o_ref.dtype)

    @pl.when(j >= GQA_QK_TILES)
    def _():
        o_ref[...] = acc.astype(o_ref.dtype)


def _proj_common_specs(layer):
    return [
        pl.BlockSpec((PROJ_TM, D_MODEL), lambda i, j: (i, 0)),
        pl.BlockSpec((2, D_MODEL), lambda i, j: (0, 0)),
        pl.BlockSpec((2, D_MODEL), lambda i, j: (0, 0)),
        pl.BlockSpec((None, D_MODEL, PROJ_TN), lambda i, j: (layer, 0, j)),
    ]


def _proj_even(xa, sc, sh, w_bf16, layer):
    return pl.pallas_call(
        _proj_even_kernel,
        grid=(NTOK // PROJ_TM, EVEN_IN // PROJ_TN),
        in_specs=_proj_common_specs(layer),
        out_specs=pl.BlockSpec((PROJ_TM, PROJ_TN), lambda i, j: (i, j)),
        out_shape=jax.ShapeDtypeStruct((NTOK, EVEN_IN), BF16),
        scratch_shapes=[pltpu.VMEM((PROJ_TM, D_MODEL), BF16)],
        compiler_params=_cparams("parallel", "arbitrary"),
        name="proj_even",
    )(xa, sc, sh, w_bf16)


def _proj_odd(xa, sc, sh, w_bf16, layer, gains, cos2, sin2):
    return pl.pallas_call(
        _proj_odd_kernel,
        grid=(NTOK // PROJ_TM, ODD_IN // PROJ_TN),
        in_specs=_proj_common_specs(layer) + [
            pl.BlockSpec((1, 1, HEAD_DIM), lambda i, j: (jnp.minimum(j // GQA_Q_TILES, 1), 0, 0)),
            pl.BlockSpec((PROJ_TM, HEAD_DIM), lambda i, j: (i, 0)),
            pl.BlockSpec((PROJ_TM, HEAD_DIM), lambda i, j: (i, 0)),
        ],
        out_specs=pl.BlockSpec((PROJ_TM, PROJ_TN), lambda i, j: (i, j)),
        out_shape=jax.ShapeDtypeStruct((NTOK, ODD_IN), BF16),
        scratch_shapes=[pltpu.VMEM((PROJ_TM, D_MODEL), BF16)],
        compiler_params=_cparams("parallel", "arbitrary"),
        name="proj_odd",
    )(xa, sc, sh, w_bf16, gains, cos2, sin2)


def _rope_tables():
    t = np.arange(SEQ)
    half = HEAD_DIM // 2
    per_axis = half // 2
    inv = ROPE_THETA ** (-np.arange(per_axis, dtype=np.float64) / per_axis)
    ang = np.concatenate([(t // GRID_W)[:, None] * inv, (t % GRID_W)[:, None] * inv], -1)
    cos = np.concatenate([np.cos(ang), np.ones((CTX, half))], 0)
    sin = np.concatenate([np.sin(ang), np.zeros((CTX, half))], 0)
    cos2 = np.concatenate([cos, cos], -1)
    sin2 = np.concatenate([-sin, sin], -1)
    return jnp.asarray(cos2, F32), jnp.asarray(sin2, F32)


NA_QBLOCKS = ROWS // NA_ROWS_PER_BLOCK
NA_LAST_KEY_ROW0 = ROWS - NA_KEY_ROWS
NA_DR = 2 * NA_KH - 1
NA_QPAIRS = NA_ROWS_PER_BLOCK // 2
NA_VT_TILE = 256
NA_VT_TILES = NTOK // NA_VT_TILE
NA_SLAB_TILES = NA_TK // NA_VT_TILE
NA_CTX_TILE = SEQ // NA_VT_TILE
NA_ONES_ROWS = 16
NA_VROWS = HEAD_DIM + NA_ONES_ROWS
NA_TAB_BOTH = 0
NA_TAB_FIRST = NA_TAB_BOTH + NA_DR - 1
NA_TAB_SECOND = NA_TAB_FIRST + NA_DR
NA_TAB_NONE = NA_TAB_SECOND + NA_DR
NA_TAB = NA_TAB_NONE + 1


def _na_col_onehot():
    c = np.arange(GRID_W)
    c_start = np.clip(c - NA_KW // 2, 0, GRID_W - NA_KW)
    inside = (c[None, :] >= c_start[:, None]) & (c[None, :] < c_start[:, None] + NA_KW)
    dc = c[None, :] - c[:, None] + NA_KW - 1
    onehot = (dc[None] == np.arange(2 * NA_KW - 1)[:, None, None]) & inside[None]
    return onehot.astype(np.float32), inside


_NA_ONEHOT, _NA_INSIDE = _na_col_onehot()


def _na_bias_table(rpb):
    t = jnp.einsum('hak,kcz->hazc', rpb.astype(F32) * LOG2E, _NA_ONEHOT, precision=lax.Precision.HIGHEST)
    t = jnp.where(_NA_INSIDE.T, t, MASK_VALUE)
    masked = jnp.full_like(t, MASK_VALUE)
    both = jnp.concatenate([t[:, 1:], t[:, :-1]], axis=-1)
    first = jnp.concatenate([t, masked], axis=-1)
    second = jnp.concatenate([masked, t], axis=-1)
    none = jnp.concatenate([masked[:, :1], masked[:, :1]], axis=-1)
    return jnp.concatenate([both, first, second, none], axis=1)


def _na_tile_index():
    idx = np.zeros((NA_QBLOCKS, NA_KEY_ROWS, NA_QPAIRS), np.int32)
    for qb in range(NA_QBLOCKS):
        kr0 = int(np.clip(NA_ROWS_PER_BLOCK * qb - NA_KH // 2, 0, NA_LAST_KEY_ROW0))
        for b in range(NA_KEY_ROWS):
            kr = kr0 + b
            for ip in range(NA_QPAIRS):
                r = NA_ROWS_PER_BLOCK * qb + 2 * ip
                lo0 = int(np.clip(r - NA_KH // 2, 0, ROWS - NA_KH))
                lo1 = int(np.clip(r + 1 - NA_KH // 2, 0, ROWS - NA_KH))
                in0 = lo0 <= kr < lo0 + NA_KH
                in1 = lo1 <= kr < lo1 + NA_KH
                dr1 = kr - (r + 1) + NA_KH - 1
                if in0 and in1:
                    idx[qb, b, ip] = NA_TAB_BOTH + dr1
                elif in0:
                    idx[qb, b, ip] = NA_TAB_FIRST + dr1 + 1
                elif in1:
                    idx[qb, b, ip] = NA_TAB_SECOND + dr1
                else:
                    idx[qb, b, ip] = NA_TAB_NONE
    return idx.reshape(-1)


_NA_TILE_INDEX = _na_tile_index()


def _na_bias_panel(idx_ref, tab_ref, qb):
    base = qb * (NA_KEY_ROWS * NA_QPAIRS)
    rows = []
    for b in range(NA_KEY_ROWS):
        tiles = [tab_ref[0, idx_ref[base + b * NA_QPAIRS + ip]] for ip in range(NA_QPAIRS)]
        rows.append(jnp.concatenate(tiles, axis=1))
    return jnp.concatenate(rows, axis=0)


def _na_kernel(idx_ref, q_ref, k_ref, vt_ref, tab_ref, o_ref, sa_ref, sb_ref, vte_ref):
    vte_ref[:, 0:HEAD_DIM, :] = vt_ref[0]
    vte_ref[:, HEAD_DIM:NA_VROWS, :] = jnp.ones((NA_VT_TILES, NA_ONES_ROWS, NA_VT_TILE), BF16)

    def q_transposed(row0):
        return q_ref[pl.ds(row0, NA_TQ), :].astype(F32).T.astype(BF16)

    def key_row0(qb):
        return jnp.clip(NA_ROWS_PER_BLOCK * qb - NA_KH // 2, 0, NA_LAST_KEY_ROW0)

    def scores(qb, buf):
        qt = q_transposed(pl.multiple_of(qb * NA_TQ, NA_TQ))
        kr0 = key_row0(qb)
        kl = k_ref[pl.ds(pl.multiple_of(kr0 * GRID_W, NA_VT_TILE), NA_TK), :]
        s_loc = _dot(kl, qt) + _na_bias_panel(idx_ref, tab_ref, qb)
        s_ctx = _dot(k_ref[pl.ds(SEQ, CTX), :], qt)
        buf[0:NA_TK] = s_loc
        buf[NA_TK:NA_TK + CTX] = s_ctx
        return jnp.maximum(jnp.max(s_loc, axis=0, keepdims=True), jnp.max(s_ctx, axis=0, keepdims=True))

    def emit(o_ext, row0):
        o = o_ext[0:HEAD_DIM] / o_ext[HEAD_DIM:HEAD_DIM + 1]
        o_ref[pl.ds(row0, NA_TQ), :] = o.T.astype(o_ref.dtype)

    def attend(qb, buf, mx):
        tile0 = key_row0(qb) * GRID_W // NA_VT_TILE
        o_ext = _dot(vte_ref[NA_CTX_TILE], jnp.exp2(buf[NA_TK:NA_TK + CTX] - mx).astype(BF16))
        for j in range(NA_SLAB_TILES):
            p = jnp.exp2(buf[j * NA_VT_TILE:(j + 1) * NA_VT_TILE] - mx).astype(BF16)
            o_ext = o_ext + _dot(vte_ref[tile0 + j], p)
        emit(o_ext, pl.multiple_of(qb * NA_TQ, NA_TQ))

    def pair(j, mx):
        mx_b = scores(2 * j + 1, sb_ref)
        attend(2 * j, sa_ref, mx)
        mx_a = scores(jnp.minimum(2 * j + 2, NA_QBLOCKS - 1), sa_ref)
        attend(2 * j + 1, sb_ref, mx_b)
        return mx_a

    lax.fori_loop(0, NA_QBLOCKS // 2, pair, scores(0, sa_ref))

    s_ctx = _dot(k_ref[pl.ds(SEQ, CTX), :], q_transposed(SEQ))
    p_ctx = jnp.exp2(s_ctx - jnp.max(s_ctx, axis=0, keepdims=True)).astype(BF16)
    emit(_dot(vte_ref[NA_CTX_TILE], p_ctx), SEQ)


def _na_attention(pe, table):
    kcol = NA_WIDTH // HEAD_DIM
    vt = pe[:, 2 * NA_WIDTH:3 * NA_WIDTH].reshape(NA_VT_TILES, NA_VT_TILE, NA_HEADS, HEAD_DIM).transpose(2, 0, 3, 1)
    return pl.pallas_call(
        _na_kernel,
        grid=(NA_HEADS,),
        in_specs=[
            pl.BlockSpec(memory_space=pltpu.SMEM),
            pl.BlockSpec((NTOK, HEAD_DIM), lambda h: (0, h)),
            pl.BlockSpec((NTOK, HEAD_DIM), lambda h: (0, kcol + h)),
            pl.BlockSpec((1, NA_VT_TILES, HEAD_DIM, NA_VT_TILE), lambda h: (h, 0, 0, 0)),
            pl.BlockSpec((1, NA_TAB, GRID_W, 2 * GRID_W), lambda h: (h, 0, 0, 0)),
        ],
        out_specs=pl.BlockSpec((NTOK, HEAD_DIM), lambda h: (0, h)),
        out_shape=jax.ShapeDtypeStruct((NTOK, NA_WIDTH), BF16),
        scratch_shapes=[
            pltpu.VMEM((NA_TK + CTX, NA_TQ), F32),
            pltpu.VMEM((NA_TK + CTX, NA_TQ), F32),
            pltpu.VMEM((NA_VT_TILES, NA_VROWS, NA_VT_TILE), BF16),
        ],
        compiler_params=_cparams("parallel"),
        name="na_attention",
    )(jnp.asarray(_NA_TILE_INDEX), pe, pe, vt, table)


S5_NCHUNK = NTOK // S5_T
S5_LAT_CHUNKS = SEQ // S5_T
S5_DROWS = S5_T * S5_GS
S5_PAIR_CH = 2 * S5_GROUP
S5_PIECES = 8
S5_PIECE_T = S5_T // S5_PIECES
S5_RE_IM_LANES = S5_GS * LANES
S5_HALF_T = S5_T // 2
S5_HALF_ROWS = S5_DROWS // 2


def _s5_fwd_chunk(j):
    return (j + S5_LAT_CHUNKS) % S5_NCHUNK


def _s5_bwd_chunk(j):
    return S5_NCHUNK - 1 - j


def _s5_kernel(uf_ref, ub_ref, bc_ref, c_ref, a_ref, yf_out, yb_out,
               b_ref, xrf_ref, xif_ref, xrb_ref, xib_ref, yfull_ref, st_ref):
    dirs = ((uf_ref, xrf_ref, xif_ref, yf_out), (ub_ref, xrb_ref, xib_ref, yb_out))

    @pl.when(pl.program_id(0) == 0)
    def _():
        st_ref[...] = jnp.zeros_like(st_ref)
        b_ref[...] = jnp.zeros_like(b_ref)
        for d in range(2):
            for s in range(S5_SLABS):
                for gs in range(S5_GS):
                    rows = slice(gs * S5_PAIR_CH, (gs + 1) * S5_PAIR_CH)
                    b_ref[d, s, rows, gs * LANES:(gs + 1) * LANES] = bc_ref[d, s, rows, 0:LANES]
                    b_ref[d, s, rows, (S5_GS + gs) * LANES:(S5_GS + gs + 1) * LANES] = (
                        bc_ref[d, s, rows, LANES:2 * LANES])

    def drive_piece(d, k):
        u_ref, xr_ref, xi_ref, _ = dirs[d]
        s, half = divmod(k, 2)
        x = _dot(u_ref[:, s * S5_SLAB_CH:(s + 1) * S5_SLAB_CH],
                 b_ref[d, s, :, half * S5_RE_IM_LANES:(half + 1) * S5_RE_IM_LANES])
        dst = xi_ref if half else xr_ref
        for gs in range(S5_GS):
            dst[s, pl.ds(gs, S5_T, stride=S5_GS), :] = x[:, gs * LANES:(gs + 1) * LANES]

    def recurrence_piece(d, k, carry):
        _, xr_ref, xi_ref, _ = dirs[d]
        ar = a_ref[d, 0]
        ai = a_ref[d, 1]
        hr, hi = carry
        for n in range(S5_PIECE_T):
            t = k * S5_PIECE_T + n
            if d == 1:
                t = S5_T - 1 - t
            rows = slice(t * S5_GS, (t + 1) * S5_GS)
            hr, hi = (ar * hr - ai * hi + xr_ref[:, rows, :],
                      ar * hi + ai * hr + xi_ref[:, rows, :])
            xr_ref[:, rows, :] = hr
            xi_ref[:, rows, :] = hi
        return hr, hi

    pairs_per_half = LANES // S5_PAIR_CH
    lane_pair = lax.broadcasted_iota(jnp.int32, (S5_HALF_T, LANES), 1) // S5_PAIR_CH

    def readout_piece(d, k):
        _, xr_ref, xi_ref, y_ref = dirs[d]
        s, th = divmod(k, 2)
        rows = slice(th * S5_HALF_ROWS, (th + 1) * S5_HALF_ROWS)
        hcat = jnp.concatenate([xr_ref[s, rows, :], xi_ref[s, rows, :]], axis=1).astype(BF16)
        yfull = _dot(hcat, c_ref[d, s])
        for hf in range(2):
            yfull_ref[d, hf] = yfull[:, hf * LANES:(hf + 1) * LANES]
        for hf in range(2):
            acc = jnp.zeros((S5_HALF_T, LANES), F32)
            for gp in range(pairs_per_half):
                gs = hf * pairs_per_half + gp
                acc = jnp.where(lane_pair == gp, yfull_ref[d, hf, pl.ds(gs, S5_HALF_T, stride=S5_GS), :], acc)
            col = s * S5_SLAB_CH + hf * LANES
            y_ref[th * S5_HALF_T:(th + 1) * S5_HALF_T, col:col + LANES] = acc

    for k in range(S5_PIECES):
        drive_piece(0, k)
    carry = (st_ref[0, 0], st_ref[0, 1])
    for k in range(S5_PIECES):
        carry = recurrence_piece(0, k, carry)
        drive_piece(1, k)
    st_ref[0, 0], st_ref[0, 1] = carry
    carry = (st_ref[1, 0], st_ref[1, 1])
    for k in range(S5_PIECES):
        carry = recurrence_piece(1, k, carry)
        readout_piece(0, k)
    st_ref[1, 0], st_ref[1, 1] = carry
    for k in range(S5_PIECES):
        readout_piece(1, k)


def _s5_tables(a_re, a_im, log_dt, b_re, b_im, c_re, c_im):
    a_re = a_re.astype(F32); a_im = a_im.astype(F32)
    dt = jnp.exp(log_dt.astype(F32))[..., None]
    mag = jnp.exp(a_re * dt)
    lb_re = mag * jnp.cos(a_im * dt)
    lb_im = mag * jnp.sin(a_im * dt)
    den = a_re * a_re + a_im * a_im
    n_re, n_im = lb_re - 1.0, lb_im
    f_re = (n_re * a_re + n_im * a_im) / den
    f_im = (n_im * a_re - n_re * a_im) / den
    bb_re = f_re[..., None] * b_re - f_im[..., None] * b_im
    bb_im = f_re[..., None] * b_im + f_im[..., None] * b_re

    def split(z):
        return z.reshape((2, S5_SLABS, S5_GS, 2) + z.shape[2:])

    eye_gl = jnp.eye(2, dtype=F32)

    def drive(bb):
        z = split(bb)
        z = jnp.einsum('dsglpc,lm->dsglcmp', z, eye_gl)
        return z.reshape(2, S5_SLABS, S5_SLAB_CH, LANES)

    bmat = jnp.concatenate([drive(bb_re), drive(bb_im)], axis=-1).astype(BF16)

    def readout(cc):
        z = split(cc.astype(F32))
        z = jnp.einsum('dshmcp,lm->dslphmc', z, eye_gl)
        return z.reshape(2, S5_SLABS, LANES, S5_SLAB_CH)

    cmat = jnp.concatenate([readout(c_re), -readout(c_im)], axis=2).astype(BF16)

    def dense(z):
        return split(z).reshape(2, S5_SLABS, S5_GS, LANES)

    avec = jnp.stack([dense(lb_re), dense(lb_im)], axis=1)
    return bmat, cmat, avec


def _s5_scan(pe, bmat, cmat, avec):
    ucol = 4 * NA_WIDTH // S5_WIDTH

    def whole(arr):
        nd = arr.ndim
        return pl.BlockSpec(arr.shape, lambda j: (0,) * nd, pipeline_mode=pl.Buffered(1))

    dense = pltpu.VMEM((S5_SLABS, S5_DROWS, LANES), F32)
    return pl.pallas_call(
        _s5_kernel,
        grid=(S5_NCHUNK,),
        in_specs=[
            pl.BlockSpec((S5_T, S5_WIDTH), lambda j: (_s5_fwd_chunk(j), ucol)),
            pl.BlockSpec((S5_T, S5_WIDTH), lambda j: (_s5_bwd_chunk(j), ucol)),
            whole(bmat), whole(cmat), whole(avec),
        ],
        out_specs=[
            pl.BlockSpec((S5_T, S5_WIDTH), lambda j: (_s5_fwd_chunk(j), 0)),
            pl.BlockSpec((S5_T, S5_WIDTH), lambda j: (_s5_bwd_chunk(j), 0)),
        ],
        out_shape=[jax.ShapeDtypeStruct((NTOK, S5_WIDTH), F32)] * 2,
        scratch_shapes=[
            pltpu.VMEM((2, S5_SLABS, S5_SLAB_CH, 2 * S5_RE_IM_LANES), BF16),
            dense, dense, dense, dense,
            pltpu.VMEM((2, 2, S5_HALF_ROWS, LANES), F32),
            pltpu.VMEM((2, 2, S5_SLABS, S5_GS, LANES), F32),
        ],
        compiler_params=_cparams("arbitrary"),
        name="s5_scan",
    )(pe, pe, bmat, cmat, avec)


def _residual_layer_norm(x, y, gt_ref, lng_ref, lnb_ref, lat):
    gt = jnp.where(lat, gt_ref[0:1, :], gt_ref[1:2, :])
    r = DN_ALPHA * x + gt * y
    mu = jnp.mean(r, axis=-1, keepdims=True)
    rc = r - mu
    var = jnp.mean(rc * rc, axis=-1, keepdims=True)
    return rc * lax.rsqrt(var + LN_EPS) * lng_ref[...] + lnb_ref[...]


def _out_even_kernel(tm, x_ref, ya_ref, ga_ref, u_ref, gb_ref, yf_ref, yb_ref, sd_ref,
                     gw_ref, gbias_ref, wo_ref, gt_ref, lng_ref, lnb_ref, o_ref):
    lat = _row_is_latent(pl.program_id(0), tm)
    za = ya_ref[...].astype(F32) * _silu(ga_ref[...].astype(F32))
    yb = u_ref[...].astype(F32) * sd_ref[...] + yf_ref[...] + yb_ref[...]
    zb = jax.nn.gelu(yb, approximate=True)
    zb = zb * jax.nn.sigmoid(_dot(zb.astype(BF16), gw_ref[...]) + gbias_ref[...])
    zb = zb * _silu(gb_ref[...].astype(F32))
    cat = jnp.concatenate([za.astype(BF16), zb.astype(BF16)], axis=1)
    y = _dot(cat, wo_ref[...])
    o_ref[...] = _residual_layer_norm(x_ref[...], y, gt_ref, lng_ref, lnb_ref, lat)


def _out_odd_kernel(tm, x_ref, o0_ref, o1_ref, g0_ref, g1_ref, wo_ref, gt_ref, lng_ref, lnb_ref, o_ref):
    lat = _row_is_latent(pl.program_id(0), tm)
    z0 = o0_ref[...].astype(F32) * _silu(g0_ref[...].astype(F32))
    z1 = o1_ref[...].astype(F32) * _silu(g1_ref[...].astype(F32))
    cat = jnp.concatenate([z0.astype(BF16), z1.astype(BF16)], axis=1)
    y = _dot(cat, wo_ref[...])
    o_ref[...] = _residual_layer_norm(x_ref[...], y, gt_ref, lng_ref, lnb_ref, lat)


def _const_spec(shape, layer=None):
    nd = len(shape)
    if layer is None:
        return pl.BlockSpec(shape, lambda i: (0,) * nd, pipeline_mode=pl.Buffered(1))
    return pl.BlockSpec((None,) + shape, lambda i: (layer,) + (0,) * nd, pipeline_mode=pl.Buffered(1))


def _out_even(xa, ya, pe, y_fwd, y_bwd, s5_d, glu_w, glu_b, w_out, layer, gt, ln_g, ln_b):
    tm = OUT_TM
    half = S5_WIDTH

    def col(c):
        return pl.BlockSpec((tm, half), lambda i: (i, c))

    return pl.pallas_call(
        functools.partial(_out_even_kernel, tm),
        grid=(NTOK // tm,),
        in_specs=[
            pl.BlockSpec((tm, D_MODEL), lambda i: (i, 0)),
            col(0),
            col(3), col(4), col(5),
            col(0),
            col(0),
            _const_spec((1, half)),
            _const_spec((half, half), layer),
            _const_spec((1, half)),
            _const_spec((D_MODEL, D_MODEL), layer),
            _const_spec((2, D_MODEL)),
            _const_spec((1, D_MODEL)),
            _const_spec((1, D_MODEL)),
        ],
        out_specs=pl.BlockSpec((tm, D_MODEL), lambda i: (i, 0)),
        out_shape=jax.ShapeDtypeStruct((NTOK, D_MODEL), F32),
        compiler_params=_cparams("parallel"),
        name="out_even",
    )(xa, ya, pe, pe, pe, y_fwd, y_bwd, s5_d.reshape(1, half), glu_w, glu_b.reshape(1, half),
      w_out, gt, ln_g.reshape(1, D_MODEL), ln_b.reshape(1, D_MODEL))


def _out_odd(xa, oa, po, w_out, layer, gt, ln_g, ln_b, last):
    tm = OUT_TM_LAST if last else OUT_TM
    nrows = SEQ if last else NTOK
    half = GQA_WIDTH // 2
    gcol = (GQA_WIDTH + 2 * GQA_KV_WIDTH) // half

    return pl.pallas_call(
        functools.partial(_out_odd_kernel, tm),
        grid=(nrows // tm,),
        in_specs=[
            pl.BlockSpec((tm, D_MODEL), lambda i: (i, 0)),
            pl.BlockSpec((tm, half), lambda i: (i, 0)),
            pl.BlockSpec((tm, half), lambda i: (i, 1)),
            pl.BlockSpec((tm, half), lambda i: (i, gcol)),
            pl.BlockSpec((tm, half), lambda i: (i, gcol + 1)),
            _const_spec((D_MODEL, D_MODEL), layer),
            _const_spec((2, D_MODEL)),
            _const_spec((1, D_MODEL)),
            _const_spec((1, D_MODEL)),
        ],
        out_specs=pl.BlockSpec((tm, D_MODEL), lambda i: (i, 0)),
        out_shape=jax.ShapeDtypeStruct((nrows, D_MODEL), F32),
        compiler_params=_cparams("parallel"),
        name="out_odd",
    )(xa, oa, oa, po, po, w_out, gt, ln_g.reshape(1, D_MODEL), ln_b.reshape(1, D_MODEL))


GQA_GROUP = GQA_HEADS // GQA_KV_HEADS
GQA_QBLOCKS = SEQ // GQA_TQ
GQA_KTILES = NTOK // GQA_TK
GQA_CTX_TILE = SEQ // GQA_TK
GQA_ONES_ROWS = 16
GQA_VROWS = HEAD_DIM + GQA_ONES_ROWS
GQA_UNROLL = 4


def _gqa_kernel(q_ref, k_ref, vt_ref, o_ref, sa_ref, sb_ref, m_ref, acc_ref, vte_ref):
    qb = pl.program_id(1)

    @pl.when(qb == 0)
    def _():
        vte_ref[:, 0:HEAD_DIM, :] = vt_ref[0]
        vte_ref[:, HEAD_DIM:GQA_VROWS, :] = jnp.ones((GQA_KTILES, GQA_ONES_ROWS, GQA_TK), BF16)

    qts = [q_ref[:, g * HEAD_DIM:(g + 1) * HEAD_DIM].astype(F32).T.astype(BF16) for g in range(GQA_GROUP)]
    is_ctx = qb == GQA_QBLOCKS

    def scores(tile, buf, g):
        k = k_ref[pl.ds(pl.multiple_of(tile * GQA_TK, GQA_TK), GQA_TK), :]
        s = _dot(k, qts[g])
        buf[g] = s
        return jnp.max(s, axis=0, keepdims=True)

    def accumulate(tile, buf, g, mx):
        m_prev = m_ref[g]
        m_cur = jnp.maximum(m_prev, mx)
        alpha = jnp.exp2(m_prev - m_cur)
        p = jnp.exp2(buf[g] - m_cur)
        acc_ref[g] = alpha * acc_ref[g] + _dot(vte_ref[tile], p.astype(BF16))
        m_ref[g] = m_cur

    def step(tile, cur, nxt, mxs):
        out = []
        for g in range(GQA_GROUP):
            out.append(scores(tile + 1, nxt, g))
            accumulate(tile, cur, g, mxs[g])
        return tuple(out)

    m_ref[...] = jnp.full_like(m_ref, MASK_VALUE)
    acc_ref[...] = jnp.zeros_like(acc_ref)
    first = jnp.where(is_ctx, GQA_CTX_TILE, 0)
    mxs = tuple(scores(first, sa_ref, g) for g in range(GQA_GROUP))

    def group(j, mxs):
        for u in range(0, GQA_UNROLL, 2):
            mxs = step(GQA_UNROLL * j + u, sa_ref, sb_ref, mxs)
            mxs = step(GQA_UNROLL * j + u + 1, sb_ref, sa_ref, mxs)
        return mxs

    mxs = lax.fori_loop(0, jnp.where(is_ctx, 0, GQA_CTX_TILE // GQA_UNROLL), group, mxs)
    for g in range(GQA_GROUP):
        accumulate(GQA_CTX_TILE, sa_ref, g, mxs[g])
        o = acc_ref[g, 0:HEAD_DIM] / acc_ref[g, HEAD_DIM:HEAD_DIM + 1]
        o_ref[:, g * HEAD_DIM:(g + 1) * HEAD_DIM] = o.T.astype(o_ref.dtype)


def _gqa_attention(po, need_ctx):
    kcol = GQA_WIDTH // HEAD_DIM
    v0 = GQA_WIDTH + GQA_KV_WIDTH
    gw = GQA_GROUP * HEAD_DIM
    nqb = GQA_QBLOCKS + (1 if need_ctx else 0)
    vt = po[:, v0:v0 + GQA_KV_WIDTH].reshape(GQA_KTILES, GQA_TK, GQA_KV_HEADS, HEAD_DIM).transpose(2, 0, 3, 1)
    return pl.pallas_call(
        _gqa_kernel,
        grid=(GQA_KV_HEADS, nqb),
        in_specs=[
            pl.BlockSpec((GQA_TQ, gw), lambda h, qb: (qb, h)),
            pl.BlockSpec((NTOK, HEAD_DIM), lambda h, qb: (0, kcol + h)),
            pl.BlockSpec((1, GQA_KTILES, HEAD_DIM, GQA_TK), lambda h, qb: (h, 0, 0, 0)),
        ],
        out_specs=pl.BlockSpec((GQA_TQ, gw), lambda h, qb: (qb, h)),
        out_shape=jax.ShapeDtypeStruct((nqb * GQA_TQ, GQA_WIDTH), BF16),
        scratch_shapes=[
            pltpu.VMEM((GQA_GROUP, GQA_TK, GQA_TQ), F32),
            pltpu.VMEM((GQA_GROUP, GQA_TK, GQA_TQ), F32),
            pltpu.VMEM((GQA_GROUP, 1, GQA_TQ), F32),
            pltpu.VMEM((GQA_GROUP, GQA_VROWS, GQA_TQ), F32),
            pltpu.VMEM((GQA_KTILES, GQA_VROWS, GQA_TK), BF16),
        ],
        compiler_params=_cparams("parallel", "arbitrary"),
        name="gqa_attention",
    )(po, po, vt)


def kernel(x, c, ctx, c_ctx, ada_w, ada_b, ln_g, ln_b, ev_w_in, ev_w_out, na_rpb, s5_a_re, s5_a_im,
           s5_log_dt, s5_b_re, s5_b_im, s5_c_re, s5_c_im, s5_d, s5_glu_w, s5_glu_b, od_w_in,
           od_w_out, q_norm_g, k_norm_g):
    assert x.shape == (1, SEQ, D_MODEL) and ctx.shape == (1, CTX, D_MODEL)
    xa = jnp.concatenate([x[0], ctx[0]], axis=0)
    mod = _ada_mod(c, c_ctx, ada_w, ada_b)
    cos2, sin2 = _rope_tables()
    ev_w_in_b = ev_w_in.astype(BF16)
    ev_w_out_b = ev_w_out.astype(BF16)
    glu_w_b = s5_glu_w.astype(BF16)
    od_w_in_b = od_w_in.astype(BF16)
    od_w_out_b = od_w_out.astype(BF16)

    for l in range(DEPTH):
        i = l // 2
        sh = mod[l, :, :D_MODEL]
        sc = mod[l, :, D_MODEL:2 * D_MODEL]
        gt = mod[l, :, 2 * D_MODEL:]
        if l % 2 == 0:
            pe = _proj_even(xa, sc, sh, ev_w_in_b, i)
            ya = _na_attention(pe, _na_bias_table(na_rpb[i]))
            bmat, cmat, avec = _s5_tables(s5_a_re[i], s5_a_im[i], s5_log_dt[i], s5_b_re[i],
                                          s5_b_im[i], s5_c_re[i], s5_c_im[i])
            y_fwd, y_bwd = _s5_scan(pe, bmat, cmat, avec)
            xa = _out_even(xa, ya, pe, y_fwd, y_bwd, s5_d[i], glu_w_b, s5_glu_b[i], ev_w_out_b, i,
                           gt, ln_g[l], ln_b[l])
        else:
            last = l == DEPTH - 1
            gains = jnp.stack([q_norm_g[i] * (ATTN_SCALE * LOG2E), k_norm_g[i]]).reshape(2, 1, HEAD_DIM)
            po = _proj_odd(xa, sc, sh, od_w_in_b, i, gains, cos2, sin2)
            oa = _gqa_attention(po, need_ctx=not last)
            xa = _out_odd(xa, oa, po, od_w_out_b, i, gt, ln_g[l], ln_b[l], last)
    return xa.reshape(1, SEQ, D_MODEL)
```

```python
import functools
import math

import numpy as np
import jax
import jax.numpy as jnp
from jax import lax
from jax.experimental import pallas as pl
from jax.experimental.pallas import tpu as pltpu

F32 = jnp.float32
BF16 = jnp.bfloat16

D_MODEL = 2048
SEQ = 8192
CTX = 256
NTOK = SEQ + CTX
DEPTH = 4
GRID_W = 64
ROWS = SEQ // GRID_W
HEAD_DIM = 128
NA_HEADS = 8
NA_WIDTH = NA_HEADS * HEAD_DIM
NA_KH = 8
NA_KW = 16
S5_WIDTH = 1024
S5_GROUP = 16
S5_GROUPS = S5_WIDTH // S5_GROUP
S5_STATE = 64
GQA_HEADS = 16
GQA_KV_HEADS = 4
GQA_WIDTH = GQA_HEADS * HEAD_DIM
GQA_KV_WIDTH = GQA_KV_HEADS * HEAD_DIM
ROPE_THETA = 10000.0
EVEN_IN = 4 * NA_WIDTH + 2 * S5_WIDTH
ODD_IN = 2 * GQA_WIDTH + 2 * GQA_KV_WIDTH
DN_ALPHA = (2 * DEPTH) ** 0.25
LN_EPS = 1e-6
RMS_EPS = 1e-6
ATTN_SCALE = HEAD_DIM ** -0.5
LOG2E = math.log2(math.e)
MASK_VALUE = -1e30

LANES = 128
SUBLANES = 8
VMEM_LIMIT_BYTES = 56 * 1024 * 1024

PROJ_TM = 1056
PROJ_TN = 512
OUT_TM = 384
OUT_TM_LAST = 512
NA_ROWS_PER_BLOCK = 4
NA_TQ = NA_ROWS_PER_BLOCK * GRID_W
NA_KEY_ROWS = 12
NA_TK = NA_KEY_ROWS * GRID_W
S5_T = 128
S5_SLABS = 4
S5_SLAB_CH = S5_WIDTH // S5_SLABS
S5_GS = 8
GQA_TQ = 256
GQA_TK = 256


def _cparams(*sem):
    return pltpu.CompilerParams(dimension_semantics=sem, vmem_limit_bytes=VMEM_LIMIT_BYTES)


def _dot(a, b):
    return jnp.dot(a, b, preferred_element_type=F32)


def _silu(x):
    return x * jax.nn.sigmoid(x)


ADA_TN = 1024


def _ada_kernel(c_ref, w_ref, b_ref, o_ref):
    s0 = _silu(c_ref[0])
    s1 = _silu(c_ref[1])
    for ch in range(ADA_TN // LANES):
        sl = slice(ch * LANES, (ch + 1) * LANES)
        w = w_ref[0, :, sl]
        b = b_ref[0, :, sl]
        o_ref[0, 0:1, sl] = jnp.sum(w * s0, axis=0, keepdims=True) + b
        o_ref[0, 1:2, sl] = jnp.sum(w * s1, axis=0, keepdims=True) + b


def _ada_mod(c, c_ctx, ada_w, ada_b):
    cb = jnp.broadcast_to(jnp.stack([c[0], c_ctx])[:, :, None], (2, D_MODEL, LANES))
    n = 3 * D_MODEL
    return pl.pallas_call(
        _ada_kernel,
        grid=(DEPTH, n // ADA_TN),
        in_specs=[
            pl.BlockSpec((2, D_MODEL, LANES), lambda l, j: (0, 0, 0)),
            pl.BlockSpec((1, D_MODEL, ADA_TN), lambda l, j: (l, 0, j)),
            pl.BlockSpec((1, 1, ADA_TN), lambda l, j: (l, 0, j)),
        ],
        out_specs=pl.BlockSpec((1, 2, ADA_TN), lambda l, j: (l, 0, j)),
        out_shape=jax.ShapeDtypeStruct((DEPTH, 2, n), F32),
        compiler_params=_cparams("parallel", "parallel"),
        name="ada_mod",
    )(cb, ada_w, ada_b.reshape(DEPTH, 1, n))


def _row_is_latent(i, tm):
    rows = i * tm + lax.broadcasted_iota(jnp.int32, (tm, 1), 0)
    return rows < SEQ


def _modulate(x_ref, sc_ref, sh_ref, h_scr):
    lat = _row_is_latent(pl.program_id(0), PROJ_TM)
    sc = jnp.where(lat, sc_ref[0:1, :], sc_ref[1:2, :])
    sh = jnp.where(lat, sh_ref[0:1, :], sh_ref[1:2, :])
    h_scr[...] = (x_ref[...] * (1.0 + sc) + sh).astype(BF16)


def _proj_even_kernel(x_ref, sc_ref, sh_ref, w_ref, o_ref, h_scr):
    @pl.when(pl.program_id(1) == 0)
    def _():
        _modulate(x_ref, sc_ref, sh_ref, h_scr)

    qscale = jnp.where(pl.program_id(1) < NA_WIDTH // PROJ_TN, ATTN_SCALE * LOG2E, 1.0)
    o_ref[...] = (_dot(h_scr[...], w_ref[...]) * qscale).astype(o_ref.dtype)


GQA_Q_TILES = GQA_WIDTH // PROJ_TN
GQA_QK_TILES = (GQA_WIDTH + GQA_KV_WIDTH) // PROJ_TN


def _proj_odd_kernel(x_ref, sc_ref, sh_ref, w_ref, gain_ref, cos_ref, sin_ref, o_ref, h_scr):
    j = pl.program_id(1)

    @pl.when(j == 0)
    def _():
        _modulate(x_ref, sc_ref, sh_ref, h_scr)

    acc = _dot(h_scr[...], w_ref[...])

    @pl.when(j < GQA_QK_TILES)
    def _():
        gain = gain_ref[0]
        cos = cos_ref[...]
        sin = sin_ref[...]
        for hh in range(PROJ_TN // HEAD_DIM):
            sl = slice(hh * HEAD_DIM, (hh + 1) * HEAD_DIM)
            z = acc[:, sl]
            z = z * lax.rsqrt(jnp.mean(z * z, axis=-1, keepdims=True) + RMS_EPS) * gain
            z = z * cos + pltpu.roll(z, HEAD_DIM // 2, axis=1) * sin
            o_ref[:, sl] = z.astype(o_ref.dtype)

    @pl.when(j >= GQA_QK_TILES)
    def _():
        o_ref[...] = acc.astype(o_ref.dtype)


def _proj_common_specs(layer):
    return [
        pl.BlockSpec((PROJ_TM, D_MODEL), lambda i, j: (i, 0)),
        pl.BlockSpec((2, D_MODEL), lambda i, j: (0, 0)),
        pl.BlockSpec((2, D_MODEL), lambda i, j: (0, 0)),
        pl.BlockSpec((None, D_MODEL, PROJ_TN), lambda i, j: (layer, 0, j)),
    ]


def _proj_even(xa, sc, sh, w_bf16, layer):
    return pl.pallas_call(
        _proj_even_kernel,
        grid=(NTOK // PROJ_TM, EVEN_IN // PROJ_TN),
        in_specs=_proj_common_specs(layer),
        out_specs=pl.BlockSpec((PROJ_TM, PROJ_TN), lambda i, j: (i, j)),
        out_shape=jax.ShapeDtypeStruct((NTOK, EVEN_IN), BF16),
        scratch_shapes=[pltpu.VMEM((PROJ_TM, D_MODEL), BF16)],
        compiler_params=_cparams("parallel", "arbitrary"),
        name="proj_even",
    )(xa, sc, sh, w_bf16)


def _proj_odd(xa, sc, sh, w_bf16, layer, gains, cos2, sin2):
    return pl.pallas_call(
        _proj_odd_kernel,
        grid=(NTOK // PROJ_TM, ODD_IN // PROJ_TN),
        in_specs=_proj_common_specs(layer) + [
            pl.BlockSpec((1, 1, HEAD_DIM), lambda i, j: (jnp.minimum(j // GQA_Q_TILES, 1), 0, 0)),
            pl.BlockSpec((PROJ_TM, HEAD_DIM), lambda i, j: (i, 0)),
            pl.BlockSpec((PROJ_TM, HEAD_DIM), lambda i, j: (i, 0)),
        ],
        out_specs=pl.BlockSpec((PROJ_TM, PROJ_TN), lambda i, j: (i, j)),
        out_shape=jax.ShapeDtypeStruct((NTOK, ODD_IN), BF16),
        scratch_shapes=[pltpu.VMEM((PROJ_TM, D_MODEL), BF16)],
        compiler_params=_cparams("parallel", "arbitrary"),
        name="proj_odd",
    )(xa, sc, sh, w_bf16, gains, cos2, sin2)


def _rope_tables():
    t = np.arange(SEQ)
    half = HEAD_DIM // 2
    per_axis = half // 2
    inv = ROPE_THETA ** (-np.arange(per_axis, dtype=np.float64) / per_axis)
    ang = np.concatenate([(t // GRID_W)[:, None] * inv, (t % GRID_W)[:, None] * inv], -1)
    cos = np.concatenate([np.cos(ang), np.ones((CTX, half))], 0)
    sin = np.concatenate([np.sin(ang), np.zeros((CTX, half))], 0)
    cos2 = np.concatenate([cos, cos], -1)
    sin2 = np.concatenate([-sin, sin], -1)
    return jnp.asarray(cos2, F32), jnp.asarray(sin2, F32)


NA_QBLOCKS = ROWS // NA_ROWS_PER_BLOCK
NA_LAST_KEY_ROW0 = ROWS - NA_KEY_ROWS
NA_DR = 2 * NA_KH - 1
NA_QPAIRS = NA_ROWS_PER_BLOCK // 2
NA_VT_TILE = 256
NA_VT_TILES = NTOK // NA_VT_TILE
NA_SLAB_TILES = NA_TK // NA_VT_TILE
NA_CTX_TILE = SEQ // NA_VT_TILE
NA_ONES_ROWS = 16
NA_VROWS = HEAD_DIM + NA_ONES_ROWS
NA_TAB_BOTH = 0
NA_TAB_FIRST = NA_TAB_BOTH + NA_DR - 1
NA_TAB_SECOND = NA_TAB_FIRST + NA_DR
NA_TAB_NONE = NA_TAB_SECOND + NA_DR
NA_TAB = NA_TAB_NONE + 1


def _na_col_onehot():
    c = np.arange(GRID_W)
    c_start = np.clip(c - NA_KW // 2, 0, GRID_W - NA_KW)
    inside = (c[None, :] >= c_start[:, None]) & (c[None, :] < c_start[:, None] + NA_KW)
    dc = c[None, :] - c[:, None] + NA_KW - 1
    onehot = (dc[None] == np.arange(2 * NA_KW - 1)[:, None, None]) & inside[None]
    return onehot.astype(np.float32), inside


_NA_ONEHOT, _NA_INSIDE = _na_col_onehot()


def _na_bias_table(rpb):
    t = jnp.einsum('hak,kcz->hazc', rpb.astype(F32) * LOG2E, _NA_ONEHOT, precision=lax.Precision.HIGHEST)
    t = jnp.where(_NA_INSIDE.T, t, MASK_VALUE)
    masked = jnp.full_like(t, MASK_VALUE)
    both = jnp.concatenate([t[:, 1:], t[:, :-1]], axis=-1)
    first = jnp.concatenate([t, masked], axis=-1)
    second = jnp.concatenate([masked, t], axis=-1)
    none = jnp.concatenate([masked[:, :1], masked[:, :1]], axis=-1)
    return jnp.concatenate([both, first, second, none], axis=1)


def _na_tile_index():
    idx = np.zeros((NA_QBLOCKS, NA_KEY_ROWS, NA_QPAIRS), np.int32)
    for qb in range(NA_QBLOCKS):
        kr0 = int(np.clip(NA_ROWS_PER_BLOCK * qb - NA_KH // 2, 0, NA_LAST_KEY_ROW0))
        for b in range(NA_KEY_ROWS):
            kr = kr0 + b
            for ip in range(NA_QPAIRS):
                r = NA_ROWS_PER_BLOCK * qb + 2 * ip
                lo0 = int(np.clip(r - NA_KH // 2, 0, ROWS - NA_KH))
                lo1 = int(np.clip(r + 1 - NA_KH // 2, 0, ROWS - NA_KH))
                in0 = lo0 <= kr < lo0 + NA_KH
                in1 = lo1 <= kr < lo1 + NA_KH
                dr1 = kr - (r + 1) + NA_KH - 1
                if in0 and in1:
                    idx[qb, b, ip] = NA_TAB_BOTH + dr1
                elif in0:
                    idx[qb, b, ip] = NA_TAB_FIRST + dr1 + 1
                elif in1:
                    idx[qb, b, ip] = NA_TAB_SECOND + dr1
                else:
                    idx[qb, b, ip] = NA_TAB_NONE
    return idx.reshape(-1)


_NA_TILE_INDEX = _na_tile_index()


def _na_bias_panel(idx_ref, tab_ref, qb):
    base = qb * (NA_KEY_ROWS * NA_QPAIRS)
    rows = []
    for b in range(NA_KEY_ROWS):
        tiles = [tab_ref[0, idx_ref[base + b * NA_QPAIRS + ip]] for ip in range(NA_QPAIRS)]
        rows.append(jnp.concatenate(tiles, axis=1))
    return jnp.concatenate(rows, axis=0)


def _na_kernel(idx_ref, q_ref, k_ref, vt_ref, tab_ref, o_ref, sa_ref, sb_ref, vte_ref):
    vte_ref[:, 0:HEAD_DIM, :] = vt_ref[0]
    vte_ref[:, HEAD_DIM:NA_VROWS, :] = jnp.ones((NA_VT_TILES, NA_ONES_ROWS, NA_VT_TILE), BF16)

    def q_transposed(qb):
        row0 = pl.multiple_of(qb * NA_TQ, NA_TQ)
        return q_ref[pl.ds(row0, NA_TQ), :].astype(F32).T.astype(BF16)

    def key_row0(qb):
        return jnp.clip(NA_ROWS_PER_BLOCK * qb - NA_KH // 2, 0, NA_LAST_KEY_ROW0)

    def scores(qb, qt, buf):
        kr0 = key_row0(qb)
        kl = k_ref[pl.ds(pl.multiple_of(kr0 * GRID_W, NA_VT_TILE), NA_TK), :]
        s_loc = _dot(kl, qt) + _na_bias_panel(idx_ref, tab_ref, qb)
        s_ctx = _dot(k_ref[pl.ds(SEQ, CTX), :], qt)
        buf[0:NA_TK] = s_loc
        buf[NA_TK:NA_TK + CTX] = s_ctx
        return jnp.maximum(jnp.max(s_loc, axis=0, keepdims=True), jnp.max(s_ctx, axis=0, keepdims=True))

    def emit(o_ext, qb):
        o = o_ext[0:HEAD_DIM] / o_ext[HEAD_DIM:HEAD_DIM + 1]
        o_ref[pl.ds(pl.multiple_of(qb * NA_TQ, NA_TQ), NA_TQ), :] = o.T.astype(o_ref.dtype)

    def attend(qb, buf, mx):
        tile0 = key_row0(qb) * GRID_W // NA_VT_TILE
        o_ext = _dot(vte_ref[NA_CTX_TILE], jnp.exp2(buf[NA_TK:NA_TK + CTX] - mx).astype(BF16))
        for j in range(NA_SLAB_TILES):
            p = jnp.exp2(buf[j * NA_VT_TILE:(j + 1) * NA_VT_TILE] - mx).astype(BF16)
            o_ext = o_ext + _dot(vte_ref[tile0 + j], p)
        return o_ext

    def pair(j, carry):
        mx, qt_odd, o_late = carry
        emit(o_late, jnp.maximum(2 * j - 1, 0))
        qt_even = q_transposed(jnp.minimum(2 * j + 2, NA_QBLOCKS - 1))
        mx_b = scores(2 * j + 1, qt_odd, sb_ref)
        emit(attend(2 * j, sa_ref, mx), 2 * j)
        qt_odd_next = q_transposed(jnp.minimum(2 * j + 3, NA_QBLOCKS))
        mx_a = scores(jnp.minimum(2 * j + 2, NA_QBLOCKS - 1), qt_even, sa_ref)
        return mx_a, qt_odd_next, attend(2 * j + 1, sb_ref, mx_b)

    first = (scores(0, q_transposed(0), sa_ref), q_transposed(1), jnp.ones((NA_VROWS, NA_TQ), F32))
    _, qt_ctx, o_late = lax.fori_loop(0, NA_QBLOCKS // 2, pair, first)
    emit(o_late, NA_QBLOCKS - 1)

    s_ctx = _dot(k_ref[pl.ds(SEQ, CTX), :], qt_ctx)
    p_ctx = jnp.exp2(s_ctx - jnp.max(s_ctx, axis=0, keepdims=True)).astype(BF16)
    emit(_dot(vte_ref[NA_CTX_TILE], p_ctx), NA_QBLOCKS)


def _na_attention(pe, table):
    kcol = NA_WIDTH // HEAD_DIM
    vt = pe[:, 2 * NA_WIDTH:3 * NA_WIDTH].reshape(NA_VT_TILES, NA_VT_TILE, NA_HEADS, HEAD_DIM).transpose(2, 0, 3, 1)
    return pl.pallas_call(
        _na_kernel,
        grid=(NA_HEADS,),
        in_specs=[
            pl.BlockSpec(memory_space=pltpu.SMEM),
            pl.BlockSpec((NTOK, HEAD_DIM), lambda h: (0, h)),
            pl.BlockSpec((NTOK, HEAD_DIM), lambda h: (0, kcol + h)),
            pl.BlockSpec((1, NA_VT_TILES, HEAD_DIM, NA_VT_TILE), lambda h: (h, 0, 0, 0)),
            pl.BlockSpec((1, NA_TAB, GRID_W, 2 * GRID_W), lambda h: (h, 0, 0, 0)),
        ],
        out_specs=pl.BlockSpec((NTOK, HEAD_DIM), lambda h: (0, h)),
        out_shape=jax.ShapeDtypeStruct((NTOK, NA_WIDTH), BF16),
        scratch_shapes=[
            pltpu.VMEM((NA_TK + CTX, NA_TQ), F32),
            pltpu.VMEM((NA_TK + CTX, NA_TQ), F32),
            pltpu.VMEM((NA_VT_TILES, NA_VROWS, NA_VT_TILE), BF16),
        ],
        compiler_params=_cparams("parallel"),
        name="na_attention",
    )(jnp.asarray(_NA_TILE_INDEX), pe, pe, vt, table)


S5_NCHUNK = NTOK // S5_T
S5_LAT_CHUNKS = SEQ // S5_T
S5_DROWS = S5_T * S5_GS
S5_PAIR_CH = 2 * S5_GROUP
S5_PIECES = 8
S5_PIECE_T = S5_T // S5_PIECES
S5_RE_IM_LANES = S5_GS * LANES
S5_HALF_T = S5_T // 2
S5_HALF_ROWS = S5_DROWS // 2


def _s5_fwd_chunk(j):
    return (j + S5_LAT_CHUNKS) % S5_NCHUNK


def _s5_bwd_chunk(j):
    return S5_NCHUNK - 1 - j


def _s5_kernel(uf_ref, ub_ref, bc_ref, c_ref, a_ref, yf_out, yb_out,
               b_ref, xrf_ref, xif_ref, xrb_ref, xib_ref, yfull_ref, st_ref):
    dirs = ((uf_ref, xrf_ref, xif_ref, yf_out), (ub_ref, xrb_ref, xib_ref, yb_out))

    @pl.when(pl.program_id(0) == 0)
    def _():
        st_ref[...] = jnp.zeros_like(st_ref)
        b_ref[...] = jnp.zeros_like(b_ref)
        for d in range(2):
            for s in range(S5_SLABS):
                for gs in range(S5_GS):
                    rows = slice(gs * S5_PAIR_CH, (gs + 1) * S5_PAIR_CH)
                    b_ref[d, s, rows, gs * LANES:(gs + 1) * LANES] = bc_ref[d, s, rows, 0:LANES]
                    b_ref[d, s, rows, (S5_GS + gs) * LANES:(S5_GS + gs + 1) * LANES] = (
                        bc_ref[d, s, rows, LANES:2 * LANES])

    def drive_piece(d, k):
        u_ref, xr_ref, xi_ref, _ = dirs[d]
        s, half = divmod(k, 2)
        x = _dot(u_ref[:, s * S5_SLAB_CH:(s + 1) * S5_SLAB_CH],
                 b_ref[d, s, :, half * S5_RE_IM_LANES:(half + 1) * S5_RE_IM_LANES])
        dst = xi_ref if half else xr_ref
        for gs in range(S5_GS):
            dst[s, pl.ds(gs, S5_T, stride=S5_GS), :] = x[:, gs * LANES:(gs + 1) * LANES]

    def recurrence_piece(d, k, carry):
        _, xr_ref, xi_ref, _ = dirs[d]
        ar = a_ref[d, 0]
        ai = a_ref[d, 1]
        hr, hi = carry
        for n in range(S5_PIECE_T):
            t = k * S5_PIECE_T + n
            if d == 1:
                t = S5_T - 1 - t
            rows = slice(t * S5_GS, (t + 1) * S5_GS)
            hr, hi = (ar * hr - ai * hi + xr_ref[:, rows, :],
                      ar * hi + ai * hr + xi_ref[:, rows, :])
            xr_ref[:, rows, :] = hr
            xi_ref[:, rows, :] = hi
        return hr, hi

    pairs_per_half = LANES // S5_PAIR_CH
    lane_pair = lax.broadcasted_iota(jnp.int32, (S5_HALF_T, LANES), 1) // S5_PAIR_CH

    def readout_piece(d, k):
        _, xr_ref, xi_ref, y_ref = dirs[d]
        s, th = divmod(k, 2)
        rows = slice(th * S5_HALF_ROWS, (th + 1) * S5_HALF_ROWS)
        hcat = jnp.concatenate([xr_ref[s, rows, :], xi_ref[s, rows, :]], axis=1).astype(BF16)
        yfull = _dot(hcat, c_ref[d, s])
        for hf in range(2):
            yfull_ref[d, hf] = yfull[:, hf * LANES:(hf + 1) * LANES]
        for hf in range(2):
            acc = jnp.zeros((S5_HALF_T, LANES), F32)
            for gp in range(pairs_per_half):
                gs = hf * pairs_per_half + gp
                acc = jnp.where(lane_pair == gp, yfull_ref[d, hf, pl.ds(gs, S5_HALF_T, stride=S5_GS), :], acc)
            col = s * S5_SLAB_CH + hf * LANES
            y_ref[th * S5_HALF_T:(th + 1) * S5_HALF_T, col:col + LANES] = acc

    for k in range(S5_PIECES):
        drive_piece(0, k)
    carry = (st_ref[0, 0], st_ref[0, 1])
    for k in range(S5_PIECES):
        carry = recurrence_piece(0, k, carry)
        drive_piece(1, k)
    st_ref[0, 0], st_ref[0, 1] = carry
    carry = (st_ref[1, 0], st_ref[1, 1])
    for k in range(S5_PIECES):
        carry = recurrence_piece(1, k, carry)
        readout_piece(0, k)
    st_ref[1, 0], st_ref[1, 1] = carry
    for k in range(S5_PIECES):
        readout_piece(1, k)


def _s5_tables(a_re, a_im, log_dt, b_re, b_im, c_re, c_im):
    a_re = a_re.astype(F32); a_im = a_im.astype(F32)
    dt = jnp.exp(log_dt.astype(F32))[..., None]
    mag = jnp.exp(a_re * dt)
    lb_re = mag * jnp.cos(a_im * dt)
    lb_im = mag * jnp.sin(a_im * dt)
    den = a_re * a_re + a_im * a_im
    n_re, n_im = lb_re - 1.0, lb_im
    f_re = (n_re * a_re + n_im * a_im) / den
    f_im = (n_im * a_re - n_re * a_im) / den
    bb_re = f_re[..., None] * b_re - f_im[..., None] * b_im
    bb_im = f_re[..., None] * b_im + f_im[..., None] * b_re

    def split(z):
        return z.reshape((2, S5_SLABS, S5_GS, 2) + z.shape[2:])

    eye_gl = jnp.eye(2, dtype=F32)

    def drive(bb):
        z = split(bb)
        z = jnp.einsum('dsglpc,lm->dsglcmp', z, eye_gl)
        return z.reshape(2, S5_SLABS, S5_SLAB_CH, LANES)

    bmat = jnp.concatenate([drive(bb_re), drive(bb_im)], axis=-1).astype(BF16)

    def readout(cc):
        z = split(cc.astype(F32))
        z = jnp.einsum('dshmcp,lm->dslphmc', z, eye_gl)
        return z.reshape(2, S5_SLABS, LANES, S5_SLAB_CH)

    cmat = jnp.concatenate([readout(c_re), -readout(c_im)], axis=2).astype(BF16)

    def dense(z):
        return split(z).reshape(2, S5_SLABS, S5_GS, LANES)

    avec = jnp.stack([dense(lb_re), dense(lb_im)], axis=1)
    return bmat, cmat, avec


def _s5_scan(pe, bmat, cmat, avec):
    ucol = 4 * NA_WIDTH // S5_WIDTH

    def whole(arr):
        nd = arr.ndim
        return pl.BlockSpec(arr.shape, lambda j: (0,) * nd, pipeline_mode=pl.Buffered(1))

    dense = pltpu.VMEM((S5_SLABS, S5_DROWS, LANES), F32)
    return pl.pallas_call(
        _s5_kernel,
        grid=(S5_NCHUNK,),
        in_specs=[
            pl.BlockSpec((S5_T, S5_WIDTH), lambda j: (_s5_fwd_chunk(j), ucol)),
            pl.BlockSpec((S5_T, S5_WIDTH), lambda j: (_s5_bwd_chunk(j), ucol)),
            whole(bmat), whole(cmat), whole(avec),
        ],
        out_specs=[
            pl.BlockSpec((S5_T, S5_WIDTH), lambda j: (_s5_fwd_chunk(j), 0)),
            pl.BlockSpec((S5_T, S5_WIDTH), lambda j: (_s5_bwd_chunk(j), 0)),
        ],
        out_shape=[jax.ShapeDtypeStruct((NTOK, S5_WIDTH), F32)] * 2,
        scratch_shapes=[
            pltpu.VMEM((2, S5_SLABS, S5_SLAB_CH, 2 * S5_RE_IM_LANES), BF16),
            dense, dense, dense, dense,
            pltpu.VMEM((2, 2, S5_HALF_ROWS, LANES), F32),
            pltpu.VMEM((2, 2, S5_SLABS, S5_GS, LANES), F32),
        ],
        compiler_params=_cparams("arbitrary"),
        name="s5_scan",
    )(pe, pe, bmat, cmat, avec)


def _residual_layer_norm(x, y, gt_ref, lng_ref, lnb_ref, lat):
    gt = jnp.where(lat, gt_ref[0:1, :], gt_ref[1:2, :])
    r = DN_ALPHA * x + gt * y
    mu = jnp.mean(r, axis=-1, keepdims=True)
    rc = r - mu
    var = jnp.mean(rc * rc, axis=-1, keepdims=True)
    return rc * lax.rsqrt(var + LN_EPS) * lng_ref[...] + lnb_ref[...]


def _out_even_kernel(tm, x_ref, ya_ref, ga_ref, u_ref, gb_ref, yf_ref, yb_ref, sd_ref,
                     gw_ref, gbias_ref, wo_ref, gt_ref, lng_ref, lnb_ref, o_ref):
    lat = _row_is_latent(pl.program_id(0), tm)
    za = ya_ref[...].astype(F32) * _silu(ga_ref[...].astype(F32))
    yb = u_ref[...].astype(F32) * sd_ref[...] + yf_ref[...] + yb_ref[...]
    zb = jax.nn.gelu(yb, approximate=True)
    zb = zb * jax.nn.sigmoid(_dot(zb.astype(BF16), gw_ref[...]) + gbias_ref[...])
    zb = zb * _silu(gb_ref[...].astype(F32))
    cat = jnp.concatenate([za.astype(BF16), zb.astype(BF16)], axis=1)
    y = _dot(cat, wo_ref[...])
    o_ref[...] = _residual_layer_norm(x_ref[...], y, gt_ref, lng_ref, lnb_ref, lat)


def _out_odd_kernel(tm, x_ref, o0_ref, o1_ref, g0_ref, g1_ref, wo_ref, gt_ref, lng_ref, lnb_ref, o_ref):
    lat = _row_is_latent(pl.program_id(0), tm)
    z0 = o0_ref[...].astype(F32) * _silu(g0_ref[...].astype(F32))
    z1 = o1_ref[...].astype(F32) * _silu(g1_ref[...].astype(F32))
    cat = jnp.concatenate([z0.astype(BF16), z1.astype(BF16)], axis=1)
    y = _dot(cat, wo_ref[...])
    o_ref[...] = _residual_layer_norm(x_ref[...], y, gt_ref, lng_ref, lnb_ref, lat)


def _const_spec(shape, layer=None):
    nd = len(shape)
    if layer is None:
        return pl.BlockSpec(shape, lambda i: (0,) * nd, pipeline_mode=pl.Buffered(1))
    return pl.BlockSpec((None,) + shape, lambda i: (layer,) + (0,) * nd, pipeline_mode=pl.Buffered(1))


def _out_even(xa, ya, pe, y_fwd, y_bwd, s5_d, glu_w, glu_b, w_out, layer, gt, ln_g, ln_b):
    tm = OUT_TM
    half = S5_WIDTH

    def col(c):
        return pl.BlockSpec((tm, half), lambda i: (i, c))

    return pl.pallas_call(
        functools.partial(_out_even_kernel, tm),
        grid=(NTOK // tm,),
        in_specs=[
            pl.BlockSpec((tm, D_MODEL), lambda i: (i, 0)),
            col(0),
            col(3), col(4), col(5),
            col(0),
            col(0),
            _const_spec((1, half)),
            _const_spec((half, half), layer),
            _const_spec((1, half)),
            _const_spec((D_MODEL, D_MODEL), layer),
            _const_spec((2, D_MODEL)),
            _const_spec((1, D_MODEL)),
            _const_spec((1, D_MODEL)),
        ],
        out_specs=pl.BlockSpec((tm, D_MODEL), lambda i: (i, 0)),
        out_shape=jax.ShapeDtypeStruct((NTOK, D_MODEL), F32),
        compiler_params=_cparams("parallel"),
        name="out_even",
    )(xa, ya, pe, pe, pe, y_fwd, y_bwd, s5_d.reshape(1, half), glu_w, glu_b.reshape(1, half),
      w_out, gt, ln_g.reshape(1, D_MODEL), ln_b.reshape(1, D_MODEL))


def _out_odd(xa, oa, po, w_out, layer, gt, ln_g, ln_b, last):
    tm = OUT_TM_LAST if last else OUT_TM
    nrows = SEQ if last else NTOK
    half = GQA_WIDTH // 2
    gcol = (GQA_WIDTH + 2 * GQA_KV_WIDTH) // half

    return pl.pallas_call(
        functools.partial(_out_odd_kernel, tm),
        grid=(nrows // tm,),
        in_specs=[
            pl.BlockSpec((tm, D_MODEL), lambda i: (i, 0)),
            pl.BlockSpec((tm, half), lambda i: (i, 0)),
            pl.BlockSpec((tm, half), lambda i: (i, 1)),
            pl.BlockSpec((tm, half), lambda i: (i, gcol)),
            pl.BlockSpec((tm, half), lambda i: (i, gcol + 1)),
            _const_spec((D_MODEL, D_MODEL), layer),
            _const_spec((2, D_MODEL)),
            _const_spec((1, D_MODEL)),
            _const_spec((1, D_MODEL)),
        ],
        out_specs=pl.BlockSpec((tm, D_MODEL), lambda i: (i, 0)),
        out_shape=jax.ShapeDtypeStruct((nrows, D_MODEL), F32),
        compiler_params=_cparams("parallel"),
        name="out_odd",
    )(xa, oa, oa, po, po, w_out, gt, ln_g.reshape(1, D_MODEL), ln_b.reshape(1, D_MODEL))


GQA_GROUP = GQA_HEADS // GQA_KV_HEADS
GQA_QBLOCKS = SEQ // GQA_TQ
GQA_KTILES = NTOK // GQA_TK
GQA_CTX_TILE = SEQ // GQA_TK
GQA_ONES_ROWS = 16
GQA_VROWS = HEAD_DIM + GQA_ONES_ROWS
GQA_UNROLL = 32


def _gqa_kernel(q_ref, k_ref, vt_ref, o_ref, sa_ref, sb_ref, m_ref, acc_ref, vte_ref):
    qb = pl.program_id(1)

    @pl.when(qb == 0)
    def _():
        vte_ref[:, 0:HEAD_DIM, :] = vt_ref[0]
        vte_ref[:, HEAD_DIM:GQA_VROWS, :] = jnp.ones((GQA_KTILES, GQA_ONES_ROWS, GQA_TK), BF16)

    qts = [q_ref[:, g * HEAD_DIM:(g + 1) * HEAD_DIM].astype(F32).T.astype(BF16) for g in range(GQA_GROUP)]
    is_ctx = qb == GQA_QBLOCKS

    def scores(tile, buf, g):
        k = k_ref[pl.ds(pl.multiple_of(tile * GQA_TK, GQA_TK), GQA_TK), :]
        s = _dot(k, qts[g])
        buf[g] = s
        return jnp.max(s, axis=0, keepdims=True)

    def accumulate(tile, buf, g, mx):
        m_prev = m_ref[g]
        m_cur = jnp.maximum(m_prev, mx)
        alpha = jnp.exp2(m_prev - m_cur)
        p = jnp.exp2(buf[g] - m_cur)
        acc_ref[g] = alpha * acc_ref[g] + _dot(vte_ref[tile], p.astype(BF16))
        m_ref[g] = m_cur

    def step(tile, cur, nxt, mxs):
        out = []
        for g in range(GQA_GROUP):
            out.append(scores(tile + 1, nxt, g))
            accumulate(tile, cur, g, mxs[g])
        return tuple(out)

    m_ref[...] = jnp.full_like(m_ref, MASK_VALUE)
    acc_ref[...] = jnp.zeros_like(acc_ref)
    first = jnp.where(is_ctx, GQA_CTX_TILE, 0)
    mxs = tuple(scores(first, sa_ref, g) for g in range(GQA_GROUP))

    def group(j, mxs):
        for u in range(0, GQA_UNROLL, 2):
            mxs = step(GQA_UNROLL * j + u, sa_ref, sb_ref, mxs)
            mxs = step(GQA_UNROLL * j + u + 1, sb_ref, sa_ref, mxs)
        return mxs

    mxs = lax.fori_loop(0, jnp.where(is_ctx, 0, GQA_CTX_TILE // GQA_UNROLL), group, mxs)
    for g in range(GQA_GROUP):
        accumulate(GQA_CTX_TILE, sa_ref, g, mxs[g])
        o = acc_ref[g, 0:HEAD_DIM] / acc_ref[g, HEAD_DIM:HEAD_DIM + 1]
        o_ref[:, g * HEAD_DIM:(g + 1) * HEAD_DIM] = o.T.astype(o_ref.dtype)


def _gqa_attention(po, need_ctx):
    kcol = GQA_WIDTH // HEAD_DIM
    v0 = GQA_WIDTH + GQA_KV_WIDTH
    gw = GQA_GROUP * HEAD_DIM
    nqb = GQA_QBLOCKS + (1 if need_ctx else 0)
    vt = po[:, v0:v0 + GQA_KV_WIDTH].reshape(GQA_KTILES, GQA_TK, GQA_KV_HEADS, HEAD_DIM).transpose(2, 0, 3, 1)
    return pl.pallas_call(
        _gqa_kernel,
        grid=(GQA_KV_HEADS, nqb),
        in_specs=[
            pl.BlockSpec((GQA_TQ, gw), lambda h, qb: (qb, h)),
            pl.BlockSpec((NTOK, HEAD_DIM), lambda h, qb: (0, kcol + h)),
            pl.BlockSpec((1, GQA_KTILES, HEAD_DIM, GQA_TK), lambda h, qb: (h, 0, 0, 0)),
        ],
        out_specs=pl.BlockSpec((GQA_TQ, gw), lambda h, qb: (qb, h)),
        out_shape=jax.ShapeDtypeStruct((nqb * GQA_TQ, GQA_WIDTH), BF16),
        scratch_shapes=[
            pltpu.VMEM((GQA_GROUP, GQA_TK, GQA_TQ), F32),
            pltpu.VMEM((GQA_GROUP, GQA_TK, GQA_TQ), F32),
            pltpu.VMEM((GQA_GROUP, 1, GQA_TQ), F32),
            pltpu.VMEM((GQA_GROUP, GQA_VROWS, GQA_TQ), F32),
            pltpu.VMEM((GQA_KTILES, GQA_VROWS, GQA_TK), BF16),
        ],
        compiler_params=_cparams("parallel", "arbitrary"),
        name="gqa_attention",
    )(po, po, vt)


def kernel(x, c, ctx, c_ctx, ada_w, ada_b, ln_g, ln_b, ev_w_in, ev_w_out, na_rpb, s5_a_re, s5_a_im,
           s5_log_dt, s5_b_re, s5_b_im, s5_c_re, s5_c_im, s5_d, s5_glu_w, s5_glu_b, od_w_in,
           od_w_out, q_norm_g, k_norm_g):
    assert x.shape == (1, SEQ, D_MODEL) and ctx.shape == (1, CTX, D_MODEL)
    xa = jnp.concatenate([x[0], ctx[0]], axis=0)
    mod = _ada_mod(c, c_ctx, ada_w, ada_b)
    cos2, sin2 = _rope_tables()
    ev_w_in_b = ev_w_in.astype(BF16)
    ev_w_out_b = ev_w_out.astype(BF16)
    glu_w_b = s5_glu_w.astype(BF16)
    od_w_in_b = od_w_in.astype(BF16)
    od_w_out_b = od_w_out.astype(BF16)

    for l in range(DEPTH):
        i = l // 2
        sh = mod[l, :, :D_MODEL]
        sc = mod[l, :, D_MODEL:2 * D_MODEL]
        gt = mod[l, :, 2 * D_MODEL:]
        if l % 2 == 0:
            pe = _proj_even(xa, sc, sh, ev_w_in_b, i)
            ya = _na_attention(pe, _na_bias_table(na_rpb[i]))
            bmat, cmat, avec = _s5_tables(s5_a_re[i], s5_a_im[i], s5_log_dt[i], s5_b_re[i],
                                          s5_b_im[i], s5_c_re[i], s5_c_im[i])
            y_fwd, y_bwd = _s5_scan(pe, bmat, cmat, avec)
            xa = _out_even(xa, ya, pe, y_fwd, y_bwd, s5_d[i], glu_w_b, s5_glu_b[i], ev_w_out_b, i,
                           gt, ln_g[l], ln_b[l])
        else:
            last = l == DEPTH - 1
            gains = jnp.stack([q_norm_g[i] * (ATTN_SCALE * LOG2E), k_norm_g[i]]).reshape(2, 1, HEAD_DIM)
            po = _proj_odd(xa, sc, sh, od_w_in_b, i, gains, cos2, sin2)
            oa = _gqa_attention(po, need_ctx=not last)
            xa = _out_odd(xa, oa, po, od_w_out_b, i, gt, ln_g[l], ln_b[l], last)
    return xa.reshape(1, SEQ, D_MODEL)
```

```python
import functools
import math

import numpy as np
import jax
import jax.numpy as jnp
from jax import lax
from jax.experimental import pallas as pl
from jax.experimental.pallas import tpu as pltpu

F32 = jnp.float32
BF16 = jnp.bfloat16

D_MODEL = 2048
SEQ = 8192
CTX = 256
NTOK = SEQ + CTX
DEPTH = 4
GRID_W = 64
ROWS = SEQ // GRID_W
HEAD_DIM = 128
NA_HEADS = 8
NA_WIDTH = NA_HEADS * HEAD_DIM
NA_KH = 8
NA_KW = 16
S5_WIDTH = 1024
S5_GROUP = 16
S5_GROUPS = S5_WIDTH // S5_GROUP
S5_STATE = 64
GQA_HEADS = 16
GQA_KV_HEADS = 4
GQA_WIDTH = GQA_HEADS * HEAD_DIM
GQA_KV_WIDTH = GQA_KV_HEADS * HEAD_DIM
ROPE_THETA = 10000.0
EVEN_IN = 4 * NA_WIDTH + 2 * S5_WIDTH
ODD_IN = 2 * GQA_WIDTH + 2 * GQA_KV_WIDTH
DN_ALPHA = (2 * DEPTH) ** 0.25
LN_EPS = 1e-6
RMS_EPS = 1e-6
ATTN_SCALE = HEAD_DIM ** -0.5
LOG2E = math.log2(math.e)
MASK_VALUE = -1e30

LANES = 128
SUBLANES = 8
VMEM_LIMIT_BYTES = 56 * 1024 * 1024

PROJ_TM = 1056
PROJ_TN_EVEN = 1024
PROJ_TN = 512
OUT_TM = 384
OUT_TM_LAST = 512
NA_ROWS_PER_BLOCK = 4
NA_TQ = NA_ROWS_PER_BLOCK * GRID_W
NA_KEY_ROWS = 12
NA_TK = NA_KEY_ROWS * GRID_W
S5_T = 256
S5_SLABS = 4
S5_SLAB_CH = S5_WIDTH // S5_SLABS
S5_GS = 8
GQA_TQ = 256
GQA_TK = 256


def _cparams(*sem):
    return pltpu.CompilerParams(dimension_semantics=sem, vmem_limit_bytes=VMEM_LIMIT_BYTES)


def _dot(a, b):
    return jnp.dot(a, b, preferred_element_type=F32)


def _silu(x):
    return x * jax.nn.sigmoid(x)


ADA_TN = 1024


def _ada_kernel(c_ref, w_ref, b_ref, o_ref):
    s0 = _silu(c_ref[0])
    s1 = _silu(c_ref[1])
    for ch in range(ADA_TN // LANES):
        sl = slice(ch * LANES, (ch + 1) * LANES)
        w = w_ref[0, :, sl]
        b = b_ref[0, :, sl]
        o_ref[0, 0:1, sl] = jnp.sum(w * s0, axis=0, keepdims=True) + b
        o_ref[0, 1:2, sl] = jnp.sum(w * s1, axis=0, keepdims=True) + b


def _ada_mod(c, c_ctx, ada_w, ada_b):
    cb = jnp.broadcast_to(jnp.stack([c[0], c_ctx])[:, :, None], (2, D_MODEL, LANES))
    n = 3 * D_MODEL
    return pl.pallas_call(
        _ada_kernel,
        grid=(DEPTH, n // ADA_TN),
        in_specs=[
            pl.BlockSpec((2, D_MODEL, LANES), lambda l, j: (0, 0, 0)),
            pl.BlockSpec((1, D_MODEL, ADA_TN), lambda l, j: (l, 0, j)),
            pl.BlockSpec((1, 1, ADA_TN), lambda l, j: (l, 0, j)),
        ],
        out_specs=pl.BlockSpec((1, 2, ADA_TN), lambda l, j: (l, 0, j)),
        out_shape=jax.ShapeDtypeStruct((DEPTH, 2, n), F32),
        compiler_params=_cparams("parallel", "parallel"),
        name="ada_mod",
    )(cb, ada_w, ada_b.reshape(DEPTH, 1, n))


def _row_is_latent(i, tm):
    rows = i * tm + lax.broadcasted_iota(jnp.int32, (tm, 1), 0)
    return rows < SEQ


def _modulate(x_ref, sc_ref, sh_ref, h_scr):
    lat = _row_is_latent(pl.program_id(0), PROJ_TM)
    sc = jnp.where(lat, sc_ref[0:1, :], sc_ref[1:2, :])
    sh = jnp.where(lat, sh_ref[0:1, :], sh_ref[1:2, :])
    h_scr[...] = (x_ref[...] * (1.0 + sc) + sh).astype(BF16)


def _proj_even_kernel(x_ref, sc_ref, sh_ref, w_ref, o_ref, h_scr):
    @pl.when(pl.program_id(1) == 0)
    def _():
        _modulate(x_ref, sc_ref, sh_ref, h_scr)

    qscale = jnp.where(pl.program_id(1) < NA_WIDTH // PROJ_TN_EVEN, ATTN_SCALE * LOG2E, 1.0)
    o_ref[...] = (_dot(h_scr[...], w_ref[...]) * qscale).astype(o_ref.dtype)


GQA_Q_TILES = GQA_WIDTH // PROJ_TN
GQA_QK_TILES = (GQA_WIDTH + GQA_KV_WIDTH) // PROJ_TN


def _proj_odd_kernel(x_ref, sc_ref, sh_ref, w_ref, gain_ref, cos_ref, sin_ref, o_ref, h_scr):
    j = pl.program_id(1)

    @pl.when(j == 0)
    def _():
        _modulate(x_ref, sc_ref, sh_ref, h_scr)

    acc = _dot(h_scr[...], w_ref[...])

    @pl.when(j < GQA_QK_TILES)
    def _():
        gain = gain_ref[0]
        cos = cos_ref[...]
        sin = sin_ref[...]
        for hh in range(PROJ_TN // HEAD_DIM):
            sl = slice(hh * HEAD_DIM, (hh + 1) * HEAD_DIM)
            z = acc[:, sl]
            z = z * lax.rsqrt(jnp.mean(z * z, axis=-1, keepdims=True) + RMS_EPS) * gain
            z = z * cos + pltpu.roll(z, HEAD_DIM // 2, axis=1) * sin
            o_ref[:, sl] = z.astype(o_ref.dtype)

    @pl.when(j >= GQA_QK_TILES)
    def _():
        o_ref[...] = acc.astype(o_ref.dtype)


def _proj_common_specs(layer, tn):
    return [
        pl.BlockSpec((PROJ_TM, D_MODEL), lambda i, j: (i, 0)),
        pl.BlockSpec((2, D_MODEL), lambda i, j: (0, 0)),
        pl.BlockSpec((2, D_MODEL), lambda i, j: (0, 0)),
        pl.BlockSpec((None, D_MODEL, tn), lambda i, j: (layer, 0, j)),
    ]


def _proj_even(xa, sc, sh, w_bf16, layer):
    return pl.pallas_call(
        _proj_even_kernel,
        grid=(NTOK // PROJ_TM, EVEN_IN // PROJ_TN_EVEN),
        in_specs=_proj_common_specs(layer, PROJ_TN_EVEN),
        out_specs=pl.BlockSpec((PROJ_TM, PROJ_TN_EVEN), lambda i, j: (i, j)),
        out_shape=jax.ShapeDtypeStruct((NTOK, EVEN_IN), BF16),
        scratch_shapes=[pltpu.VMEM((PROJ_TM, D_MODEL), BF16)],
        compiler_params=_cparams("parallel", "arbitrary"),
        name="proj_even",
    )(xa, sc, sh, w_bf16)


def _proj_odd(xa, sc, sh, w_bf16, layer, gains, cos2, sin2):
    return pl.pallas_call(
        _proj_odd_kernel,
        grid=(NTOK // PROJ_TM, ODD_IN // PROJ_TN),
        in_specs=_proj_common_specs(layer, PROJ_TN) + [
            pl.BlockSpec((1, 1, HEAD_DIM), lambda i, j: (jnp.minimum(j // GQA_Q_TILES, 1), 0, 0)),
            pl.BlockSpec((PROJ_TM, HEAD_DIM), lambda i, j: (i, 0)),
            pl.BlockSpec((PROJ_TM, HEAD_DIM), lambda i, j: (i, 0)),
        ],
        out_specs=pl.BlockSpec((PROJ_TM, PROJ_TN), lambda i, j: (i, j)),
        out_shape=jax.ShapeDtypeStruct((NTOK, ODD_IN), BF16),
        scratch_shapes=[pltpu.VMEM((PROJ_TM, D_MODEL), BF16)],
        compiler_params=_cparams("parallel", "arbitrary"),
        name="proj_odd",
    )(xa, sc, sh, w_bf16, gains, cos2, sin2)


def _rope_tables():
    t = np.arange(SEQ)
    half = HEAD_DIM // 2
    per_axis = half // 2
    inv = ROPE_THETA ** (-np.arange(per_axis, dtype=np.float64) / per_axis)
    ang = np.concatenate([(t // GRID_W)[:, None] * inv, (t % GRID_W)[:, None] * inv], -1)
    cos = np.concatenate([np.cos(ang), np.ones((CTX, half))], 0)
    sin = np.concatenate([np.sin(ang), np.zeros((CTX, half))], 0)
    cos2 = np.concatenate([cos, cos], -1)
    sin2 = np.concatenate([-sin, sin], -1)
    return jnp.asarray(cos2, F32), jnp.asarray(sin2, F32)


NA_QBLOCKS = ROWS // NA_ROWS_PER_BLOCK
NA_LAST_KEY_ROW0 = ROWS - NA_KEY_ROWS
NA_DR = 2 * NA_KH - 1
NA_QPAIRS = NA_ROWS_PER_BLOCK // 2
NA_VT_TILE = 256
NA_VT_TILES = NTOK // NA_VT_TILE
NA_SLAB_TILES = NA_TK // NA_VT_TILE
NA_CTX_TILE = SEQ // NA_VT_TILE
NA_PAIRS_PER_TRIP = 4
NA_ONES_ROWS = 16
NA_VROWS = HEAD_DIM + NA_ONES_ROWS
NA_TAB_BOTH = 0
NA_TAB_FIRST = NA_TAB_BOTH + NA_DR - 1
NA_TAB_SECOND = NA_TAB_FIRST + NA_DR
NA_TAB_NONE = NA_TAB_SECOND + NA_DR
NA_TAB = NA_TAB_NONE + 1


def _na_col_onehot():
    c = np.arange(GRID_W)
    c_start = np.clip(c - NA_KW // 2, 0, GRID_W - NA_KW)
    inside = (c[None, :] >= c_start[:, None]) & (c[None, :] < c_start[:, None] + NA_KW)
    dc = c[None, :] - c[:, None] + NA_KW - 1
    onehot = (dc[None] == np.arange(2 * NA_KW - 1)[:, None, None]) & inside[None]
    return onehot.astype(np.float32), inside


_NA_ONEHOT, _NA_INSIDE = _na_col_onehot()


def _na_bias_table(rpb):
    t = jnp.einsum('hak,kcz->hazc', rpb.astype(F32) * LOG2E, _NA_ONEHOT, precision=lax.Precision.HIGHEST)
    t = jnp.where(_NA_INSIDE.T, t, MASK_VALUE)
    masked = jnp.full_like(t, MASK_VALUE)
    both = jnp.concatenate([t[:, 1:], t[:, :-1]], axis=-1)
    first = jnp.concatenate([t, masked], axis=-1)
    second = jnp.concatenate([masked, t], axis=-1)
    none = jnp.concatenate([masked[:, :1], masked[:, :1]], axis=-1)
    return jnp.concatenate([both, first, second, none], axis=1)


def _na_tile_index():
    idx = np.zeros((NA_QBLOCKS, NA_KEY_ROWS, NA_QPAIRS), np.int32)
    for qb in range(NA_QBLOCKS):
        kr0 = int(np.clip(NA_ROWS_PER_BLOCK * qb - NA_KH // 2, 0, NA_LAST_KEY_ROW0))
        for b in range(NA_KEY_ROWS):
            kr = kr0 + b
            for ip in range(NA_QPAIRS):
                r = NA_ROWS_PER_BLOCK * qb + 2 * ip
                lo0 = int(np.clip(r - NA_KH // 2, 0, ROWS - NA_KH))
                lo1 = int(np.clip(r + 1 - NA_KH // 2, 0, ROWS - NA_KH))
                in0 = lo0 <= kr < lo0 + NA_KH
                in1 = lo1 <= kr < lo1 + NA_KH
                dr1 = kr - (r + 1) + NA_KH - 1
                if in0 and in1:
                    idx[qb, b, ip] = NA_TAB_BOTH + dr1
                elif in0:
                    idx[qb, b, ip] = NA_TAB_FIRST + dr1 + 1
                elif in1:
                    idx[qb, b, ip] = NA_TAB_SECOND + dr1
                else:
                    idx[qb, b, ip] = NA_TAB_NONE
    return idx.reshape(-1)


_NA_TILE_INDEX = _na_tile_index()


def _na_bias_panel(idx_ref, tab_ref, qb):
    base = qb * (NA_KEY_ROWS * NA_QPAIRS)
    rows = []
    for b in range(NA_KEY_ROWS):
        tiles = [tab_ref[0, idx_ref[base + b * NA_QPAIRS + ip]] for ip in range(NA_QPAIRS)]
        rows.append(jnp.concatenate(tiles, axis=1))
    return jnp.concatenate(rows, axis=0)


def _na_kernel(idx_ref, q_ref, k_ref, vt_ref, tab_ref, o_ref, sa_ref, sb_ref, vte_ref):
    vte_ref[:, 0:HEAD_DIM, :] = vt_ref[0]
    vte_ref[:, HEAD_DIM:NA_VROWS, :] = jnp.ones((NA_VT_TILES, NA_ONES_ROWS, NA_VT_TILE), BF16)

    def q_transposed(qb):
        row0 = pl.multiple_of(qb * NA_TQ, NA_TQ)
        return q_ref[pl.ds(row0, NA_TQ), :].astype(F32).T.astype(BF16)

    def key_row0(qb):
        return jnp.clip(NA_ROWS_PER_BLOCK * qb - NA_KH // 2, 0, NA_LAST_KEY_ROW0)

    def scores(qb, qt, buf):
        kr0 = key_row0(qb)
        kl = k_ref[pl.ds(pl.multiple_of(kr0 * GRID_W, NA_VT_TILE), NA_TK), :]
        s_loc = _dot(kl, qt) + _na_bias_panel(idx_ref, tab_ref, qb)
        s_ctx = _dot(k_ref[pl.ds(SEQ, CTX), :], qt)
        buf[0:NA_TK] = s_loc
        buf[NA_TK:NA_TK + CTX] = s_ctx
        return jnp.maximum(jnp.max(s_loc, axis=0, keepdims=True), jnp.max(s_ctx, axis=0, keepdims=True))

    def emit(o_ext, qb):
        o = o_ext[0:HEAD_DIM] / o_ext[HEAD_DIM:HEAD_DIM + 1]
        o_ref[pl.ds(pl.multiple_of(qb * NA_TQ, NA_TQ), NA_TQ), :] = o.T.astype(o_ref.dtype)

    def attend(qb, buf, mx):
        tile0 = key_row0(qb) * GRID_W // NA_VT_TILE
        o_ext = _dot(vte_ref[NA_CTX_TILE], jnp.exp2(buf[NA_TK:NA_TK + CTX] - mx).astype(BF16))
        for j in range(NA_SLAB_TILES):
            p = jnp.exp2(buf[j * NA_VT_TILE:(j + 1) * NA_VT_TILE] - mx).astype(BF16)
            o_ext = o_ext + _dot(vte_ref[tile0 + j], p)
        return o_ext

    def pair(j, carry):
        mx, qt_odd, o_late = carry
        emit(o_late, jnp.maximum(2 * j - 1, 0))
        qt_even = q_transposed(jnp.minimum(2 * j + 2, NA_QBLOCKS - 1))
        mx_b = scores(2 * j + 1, qt_odd, sb_ref)
        emit(attend(2 * j, sa_ref, mx), 2 * j)
        qt_odd_next = q_transposed(jnp.minimum(2 * j + 3, NA_QBLOCKS))
        mx_a = scores(jnp.minimum(2 * j + 2, NA_QBLOCKS - 1), qt_even, sa_ref)
        return mx_a, qt_odd_next, attend(2 * j + 1, sb_ref, mx_b)

    def trip(t, carry):
        for u in range(NA_PAIRS_PER_TRIP):
            carry = pair(t * NA_PAIRS_PER_TRIP + u, carry)
        return carry

    first = (scores(0, q_transposed(0), sa_ref), q_transposed(1), jnp.ones((NA_VROWS, NA_TQ), F32))
    _, qt_ctx, o_late = lax.fori_loop(0, NA_QBLOCKS // (2 * NA_PAIRS_PER_TRIP), trip, first)
    emit(o_late, NA_QBLOCKS - 1)

    s_ctx = _dot(k_ref[pl.ds(SEQ, CTX), :], qt_ctx)
    p_ctx = jnp.exp2(s_ctx - jnp.max(s_ctx, axis=0, keepdims=True)).astype(BF16)
    emit(_dot(vte_ref[NA_CTX_TILE], p_ctx), NA_QBLOCKS)


def _na_attention(pe, table):
    kcol = NA_WIDTH // HEAD_DIM
    vt = pe[:, 2 * NA_WIDTH:3 * NA_WIDTH].reshape(NA_VT_TILES, NA_VT_TILE, NA_HEADS, HEAD_DIM).transpose(2, 0, 3, 1)
    return pl.pallas_call(
        _na_kernel,
        grid=(NA_HEADS,),
        in_specs=[
            pl.BlockSpec(memory_space=pltpu.SMEM),
            pl.BlockSpec((NTOK, HEAD_DIM), lambda h: (0, h)),
            pl.BlockSpec((NTOK, HEAD_DIM), lambda h: (0, kcol + h)),
            pl.BlockSpec((1, NA_VT_TILES, HEAD_DIM, NA_VT_TILE), lambda h: (h, 0, 0, 0)),
            pl.BlockSpec((1, NA_TAB, GRID_W, 2 * GRID_W), lambda h: (h, 0, 0, 0)),
        ],
        out_specs=pl.BlockSpec((NTOK, HEAD_DIM), lambda h: (0, h)),
        out_shape=jax.ShapeDtypeStruct((NTOK, NA_WIDTH), BF16),
        scratch_shapes=[
            pltpu.VMEM((NA_TK + CTX, NA_TQ), F32),
            pltpu.VMEM((NA_TK + CTX, NA_TQ), F32),
            pltpu.VMEM((NA_VT_TILES, NA_VROWS, NA_VT_TILE), BF16),
        ],
        compiler_params=_cparams("parallel"),
        name="na_attention",
    )(jnp.asarray(_NA_TILE_INDEX), pe, pe, vt, table)


S5_NCHUNK = NTOK // S5_T
S5_LAT_CHUNKS = SEQ // S5_T
S5_DROWS = S5_T * S5_GS
S5_PAIR_CH = 2 * S5_GROUP
S5_PIECES = 8
S5_PIECE_T = S5_T // S5_PIECES
S5_RE_IM_LANES = S5_GS * LANES
S5_HALF_T = S5_T // 2
S5_HALF_ROWS = S5_DROWS // 2


def _s5_fwd_chunk(j):
    return (j + S5_LAT_CHUNKS) % S5_NCHUNK


def _s5_bwd_chunk(j):
    return S5_NCHUNK - 1 - j


def _s5_kernel(uf_ref, ub_ref, bc_ref, c_ref, a_ref, yf_out, yb_out,
               b_ref, xrf_ref, xif_ref, xrb_ref, xib_ref, yfull_ref, st_ref):
    dirs = ((uf_ref, xrf_ref, xif_ref, yf_out), (ub_ref, xrb_ref, xib_ref, yb_out))

    @pl.when(pl.program_id(0) == 0)
    def _():
        st_ref[...] = jnp.zeros_like(st_ref)
        b_ref[...] = jnp.zeros_like(b_ref)
        for d in range(2):
            for s in range(S5_SLABS):
                for gs in range(S5_GS):
                    rows = slice(gs * S5_PAIR_CH, (gs + 1) * S5_PAIR_CH)
                    b_ref[d, s, rows, gs * LANES:(gs + 1) * LANES] = bc_ref[d, s, rows, 0:LANES]
                    b_ref[d, s, rows, (S5_GS + gs) * LANES:(S5_GS + gs + 1) * LANES] = (
                        bc_ref[d, s, rows, LANES:2 * LANES])

    def drive_piece(d, k):
        u_ref, xr_ref, xi_ref, _ = dirs[d]
        s, half = divmod(k, 2)
        x = _dot(u_ref[:, s * S5_SLAB_CH:(s + 1) * S5_SLAB_CH],
                 b_ref[d, s, :, half * S5_RE_IM_LANES:(half + 1) * S5_RE_IM_LANES])
        dst = xi_ref if half else xr_ref
        for gs in range(S5_GS):
            dst[s, pl.ds(gs, S5_T, stride=S5_GS), :] = x[:, gs * LANES:(gs + 1) * LANES]

    def recurrence_piece(d, k, carry):
        _, xr_ref, xi_ref, _ = dirs[d]
        ar = a_ref[d, 0]
        ai = a_ref[d, 1]
        hr, hi = carry
        for n in range(S5_PIECE_T):
            t = k * S5_PIECE_T + n
            if d == 1:
                t = S5_T - 1 - t
            rows = slice(t * S5_GS, (t + 1) * S5_GS)
            hr, hi = (ar * hr - ai * hi + xr_ref[:, rows, :],
                      ar * hi + ai * hr + xi_ref[:, rows, :])
            xr_ref[:, rows, :] = hr
            xi_ref[:, rows, :] = hi
        return hr, hi

    pairs_per_half = LANES // S5_PAIR_CH
    lane_pair = lax.broadcasted_iota(jnp.int32, (S5_HALF_T, LANES), 1) // S5_PAIR_CH

    def readout_piece(d, k):
        _, xr_ref, xi_ref, y_ref = dirs[d]
        s, th = divmod(k, 2)
        rows = slice(th * S5_HALF_ROWS, (th + 1) * S5_HALF_ROWS)
        hcat = jnp.concatenate([xr_ref[s, rows, :], xi_ref[s, rows, :]], axis=1).astype(BF16)
        yfull = _dot(hcat, c_ref[d, s])
        for hf in range(2):
            yfull_ref[d, hf] = yfull[:, hf * LANES:(hf + 1) * LANES]
        for hf in range(2):
            acc = jnp.zeros((S5_HALF_T, LANES), F32)
            for gp in range(pairs_per_half):
                gs = hf * pairs_per_half + gp
                acc = jnp.where(lane_pair == gp, yfull_ref[d, hf, pl.ds(gs, S5_HALF_T, stride=S5_GS), :], acc)
            col = s * S5_SLAB_CH + hf * LANES
            y_ref[th * S5_HALF_T:(th + 1) * S5_HALF_T, col:col + LANES] = acc

    for k in range(S5_PIECES):
        drive_piece(0, k)
    carry = (st_ref[0, 0], st_ref[0, 1])
    for k in range(S5_PIECES):
        carry = recurrence_piece(0, k, carry)
        drive_piece(1, k)
    st_ref[0, 0], st_ref[0, 1] = carry
    carry = (st_ref[1, 0], st_ref[1, 1])
    for k in range(S5_PIECES):
        carry = recurrence_piece(1, k, carry)
        readout_piece(0, k)
    st_ref[1, 0], st_ref[1, 1] = carry
    for k in range(S5_PIECES):
        readout_piece(1, k)


def _s5_tables(a_re, a_im, log_dt, b_re, b_im, c_re, c_im):
    a_re = a_re.astype(F32); a_im = a_im.astype(F32)
    dt = jnp.exp(log_dt.astype(F32))[..., None]
    mag = jnp.exp(a_re * dt)
    lb_re = mag * jnp.cos(a_im * dt)
    lb_im = mag * jnp.sin(a_im * dt)
    den = a_re * a_re + a_im * a_im
    n_re, n_im = lb_re - 1.0, lb_im
    f_re = (n_re * a_re + n_im * a_im) / den
    f_im = (n_im * a_re - n_re * a_im) / den
    bb_re = f_re[..., None] * b_re - f_im[..., None] * b_im
    bb_im = f_re[..., None] * b_im + f_im[..., None] * b_re

    def split(z):
        return z.reshape((2, S5_SLABS, S5_GS, 2) + z.shape[2:])

    eye_gl = jnp.eye(2, dtype=F32)

    def drive(bb):
        z = split(bb)
        z = jnp.einsum('dsglpc,lm->dsglcmp', z, eye_gl)
        return z.reshape(2, S5_SLABS, S5_SLAB_CH, LANES)

    bmat = jnp.concatenate([drive(bb_re), drive(bb_im)], axis=-1).astype(BF16)

    def readout(cc):
        z = split(cc.astype(F32))
        z = jnp.einsum('dshmcp,lm->dslphmc', z, eye_gl)
        return z.reshape(2, S5_SLABS, LANES, S5_SLAB_CH)

    cmat = jnp.concatenate([readout(c_re), -readout(c_im)], axis=2).astype(BF16)

    def dense(z):
        return split(z).reshape(2, S5_SLABS, S5_GS, LANES)

    avec = jnp.stack([dense(lb_re), dense(lb_im)], axis=1)
    return bmat, cmat, avec


def _s5_scan(pe, bmat, cmat, avec):
    ucol = 4 * NA_WIDTH // S5_WIDTH

    def whole(arr):
        nd = arr.ndim
        return pl.BlockSpec(arr.shape, lambda j: (0,) * nd, pipeline_mode=pl.Buffered(1))

    dense = pltpu.VMEM((S5_SLABS, S5_DROWS, LANES), F32)
    return pl.pallas_call(
        _s5_kernel,
        grid=(S5_NCHUNK,),
        in_specs=[
            pl.BlockSpec((S5_T, S5_WIDTH), lambda j: (_s5_fwd_chunk(j), ucol)),
            pl.BlockSpec((S5_T, S5_WIDTH), lambda j: (_s5_bwd_chunk(j), ucol)),
            whole(bmat), whole(cmat), whole(avec),
        ],
        out_specs=[
            pl.BlockSpec((S5_T, S5_WIDTH), lambda j: (_s5_fwd_chunk(j), 0)),
            pl.BlockSpec((S5_T, S5_WIDTH), lambda j: (_s5_bwd_chunk(j), 0)),
        ],
        out_shape=[jax.ShapeDtypeStruct((NTOK, S5_WIDTH), F32)] * 2,
        scratch_shapes=[
            pltpu.VMEM((2, S5_SLABS, S5_SLAB_CH, 2 * S5_RE_IM_LANES), BF16),
            dense, dense, dense, dense,
            pltpu.VMEM((2, 2, S5_HALF_ROWS, LANES), F32),
            pltpu.VMEM((2, 2, S5_SLABS, S5_GS, LANES), F32),
        ],
        compiler_params=_cparams("arbitrary"),
        name="s5_scan",
    )(pe, pe, bmat, cmat, avec)


def _residual_layer_norm(x, y, gt_ref, lng_ref, lnb_ref, lat):
    gt = jnp.where(lat, gt_ref[0:1, :], gt_ref[1:2, :])
    r = DN_ALPHA * x + gt * y
    mu = jnp.mean(r, axis=-1, keepdims=True)
    rc = r - mu
    var = jnp.mean(rc * rc, axis=-1, keepdims=True)
    return rc * lax.rsqrt(var + LN_EPS) * lng_ref[...] + lnb_ref[...]


def _out_even_kernel(tm, x_ref, ya_ref, ga_ref, u_ref, gb_ref, yf_ref, yb_ref, sd_ref,
                     gw_ref, gbias_ref, wo_ref, gt_ref, lng_ref, lnb_ref, o_ref):
    lat = _row_is_latent(pl.program_id(0), tm)
    za = ya_ref[...].astype(F32) * _silu(ga_ref[...].astype(F32))
    yb = u_ref[...].astype(F32) * sd_ref[...] + yf_ref[...] + yb_ref[...]
    zb = jax.nn.gelu(yb, approximate=True)
    zb = zb * jax.nn.sigmoid(_dot(zb.astype(BF16), gw_ref[...]) + gbias_ref[...])
    zb = zb * _silu(gb_ref[...].astype(F32))
    cat = jnp.concatenate([za.astype(BF16), zb.astype(BF16)], axis=1)
    y = _dot(cat, wo_ref[...])
    o_ref[...] = _residual_layer_norm(x_ref[...], y, gt_ref, lng_ref, lnb_ref, lat)


def _out_odd_kernel(tm, x_ref, o0_ref, o1_ref, g0_ref, g1_ref, wo_ref, gt_ref, lng_ref, lnb_ref, o_ref):
    lat = _row_is_latent(pl.program_id(0), tm)
    z0 = o0_ref[...].astype(F32) * _silu(g0_ref[...].astype(F32))
    z1 = o1_ref[...].astype(F32) * _silu(g1_ref[...].astype(F32))
    cat = jnp.concatenate([z0.astype(BF16), z1.astype(BF16)], axis=1)
    y = _dot(cat, wo_ref[...])
    o_ref[...] = _residual_layer_norm(x_ref[...], y, gt_ref, lng_ref, lnb_ref, lat)


def _const_spec(shape, layer=None):
    nd = len(shape)
    if layer is None:
        return pl.BlockSpec(shape, lambda i: (0,) * nd, pipeline_mode=pl.Buffered(1))
    return pl.BlockSpec((None,) + shape, lambda i: (layer,) + (0,) * nd, pipeline_mode=pl.Buffered(1))


def _out_even(xa, ya, pe, y_fwd, y_bwd, s5_d, glu_w, glu_b, w_out, layer, gt, ln_g, ln_b):
    tm = OUT_TM
    half = S5_WIDTH

    def col(c):
        return pl.BlockSpec((tm, half), lambda i: (i, c))

    return pl.pallas_call(
        functools.partial(_out_even_kernel, tm),
        grid=(NTOK // tm,),
        in_specs=[
            pl.BlockSpec((tm, D_MODEL), lambda i: (i, 0)),
            col(0),
            col(3), col(4), col(5),
            col(0),
            col(0),
            _const_spec((1, half)),
            _const_spec((half, half), layer),
            _const_spec((1, half)),
            _const_spec((D_MODEL, D_MODEL), layer),
            _const_spec((2, D_MODEL)),
            _const_spec((1, D_MODEL)),
            _const_spec((1, D_MODEL)),
        ],
        out_specs=pl.BlockSpec((tm, D_MODEL), lambda i: (i, 0)),
        out_shape=jax.ShapeDtypeStruct((NTOK, D_MODEL), F32),
        compiler_params=_cparams("parallel"),
        name="out_even",
    )(xa, ya, pe, pe, pe, y_fwd, y_bwd, s5_d.reshape(1, half), glu_w, glu_b.reshape(1, half),
      w_out, gt, ln_g.reshape(1, D_MODEL), ln_b.reshape(1, D_MODEL))


def _out_odd(xa, oa, po, w_out, layer, gt, ln_g, ln_b, last):
    tm = OUT_TM_LAST if last else OUT_TM
    nrows = SEQ if last else NTOK
    half = GQA_WIDTH // 2
    gcol = (GQA_WIDTH + 2 * GQA_KV_WIDTH) // half

    return pl.pallas_call(
        functools.partial(_out_odd_kernel, tm),
        grid=(nrows // tm,),
        in_specs=[
            pl.BlockSpec((tm, D_MODEL), lambda i: (i, 0)),
            pl.BlockSpec((tm, half), lambda i: (i, 0)),
            pl.BlockSpec((tm, half), lambda i: (i, 1)),
            pl.BlockSpec((tm, half), lambda i: (i, gcol)),
            pl.BlockSpec((tm, half), lambda i: (i, gcol + 1)),
            _const_spec((D_MODEL, D_MODEL), layer),
            _const_spec((2, D_MODEL)),
            _const_spec((1, D_MODEL)),
            _const_spec((1, D_MODEL)),
        ],
        out_specs=pl.BlockSpec((tm, D_MODEL), lambda i: (i, 0)),
        out_shape=jax.ShapeDtypeStruct((nrows, D_MODEL), F32),
        compiler_params=_cparams("parallel"),
        name="out_odd",
    )(xa, oa, oa, po, po, w_out, gt, ln_g.reshape(1, D_MODEL), ln_b.reshape(1, D_MODEL))


GQA_GROUP = GQA_HEADS // GQA_KV_HEADS
GQA_QBLOCKS = SEQ // GQA_TQ
GQA_KTILES = NTOK // GQA_TK
GQA_CTX_TILE = SEQ // GQA_TK
GQA_ONES_ROWS = 16
GQA_VROWS = HEAD_DIM + GQA_ONES_ROWS
GQA_UNROLL = 32


def _gqa_kernel(q_ref, k_ref, vt_ref, o_ref, sa_ref, sb_ref, m_ref, acc_ref, vte_ref):
    qb = pl.program_id(1)

    @pl.when(qb == 0)
    def _():
        vte_ref[:, 0:HEAD_DIM, :] = vt_ref[0]
        vte_ref[:, HEAD_DIM:GQA_VROWS, :] = jnp.ones((GQA_KTILES, GQA_ONES_ROWS, GQA_TK), BF16)

    qts = [q_ref[:, g * HEAD_DIM:(g + 1) * HEAD_DIM].astype(F32).T.astype(BF16) for g in range(GQA_GROUP)]
    is_ctx = qb == GQA_QBLOCKS

    def scores(tile, buf, g):
        k = k_ref[pl.ds(pl.multiple_of(tile * GQA_TK, GQA_TK), GQA_TK), :]
        s = _dot(k, qts[g])
        buf[g] = s
        return jnp.max(s, axis=0, keepdims=True)

    def accumulate(tile, buf, g, mx):
        m_prev = m_ref[g]
        m_cur = jnp.maximum(m_prev, mx)
        alpha = jnp.exp2(m_prev - m_cur)
        p = jnp.exp2(buf[g] - m_cur)
        acc_ref[g] = alpha * acc_ref[g] + _dot(vte_ref[tile], p.astype(BF16))
        m_ref[g] = m_cur

    def step(tile, cur, nxt, mxs):
        out = []
        for g in range(GQA_GROUP):
            out.append(scores(tile + 1, nxt, g))
            accumulate(tile, cur, g, mxs[g])
        return tuple(out)

    m_ref[...] = jnp.full_like(m_ref, MASK_VALUE)
    acc_ref[...] = jnp.zeros_like(acc_ref)
    first = jnp.where(is_ctx, GQA_CTX_TILE, 0)
    mxs = tuple(scores(first, sa_ref, g) for g in range(GQA_GROUP))

    def group(j, mxs):
        for u in range(0, GQA_UNROLL, 2):
            mxs = step(GQA_UNROLL * j + u, sa_ref, sb_ref, mxs)
            mxs = step(GQA_UNROLL * j + u + 1, sb_ref, sa_ref, mxs)
        return mxs

    mxs = lax.fori_loop(0, jnp.where(is_ctx, 0, GQA_CTX_TILE // GQA_UNROLL), group, mxs)
    for g in range(GQA_GROUP):
        accumulate(GQA_CTX_TILE, sa_ref, g, mxs[g])
        o = acc_ref[g, 0:HEAD_DIM] / acc_ref[g, HEAD_DIM:HEAD_DIM + 1]
        o_ref[:, g * HEAD_DIM:(g + 1) * HEAD_DIM] = o.T.astype(o_ref.dtype)


def _gqa_attention(po, need_ctx):
    kcol = GQA_WIDTH // HEAD_DIM
    v0 = GQA_WIDTH + GQA_KV_WIDTH
    gw = GQA_GROUP * HEAD_DIM
    nqb = GQA_QBLOCKS + (1 if need_ctx else 0)
    vt = po[:, v0:v0 + GQA_KV_WIDTH].reshape(GQA_KTILES, GQA_TK, GQA_KV_HEADS, HEAD_DIM).transpose(2, 0, 3, 1)
    return pl.pallas_call(
        _gqa_kernel,
        grid=(GQA_KV_HEADS, nqb),
        in_specs=[
            pl.BlockSpec((GQA_TQ, gw), lambda h, qb: (qb, h)),
            pl.BlockSpec((NTOK, HEAD_DIM), lambda h, qb: (0, kcol + h)),
            pl.BlockSpec((1, GQA_KTILES, HEAD_DIM, GQA_TK), lambda h, qb: (h, 0, 0, 0)),
        ],
        out_specs=pl.BlockSpec((GQA_TQ, gw), lambda h, qb: (qb, h)),
        out_shape=jax.ShapeDtypeStruct((nqb * GQA_TQ, GQA_WIDTH), BF16),
        scratch_shapes=[
            pltpu.VMEM((GQA_GROUP, GQA_TK, GQA_TQ), F32),
            pltpu.VMEM((GQA_GROUP, GQA_TK, GQA_TQ), F32),
            pltpu.VMEM((GQA_GROUP, 1, GQA_TQ), F32),
            pltpu.VMEM((GQA_GROUP, GQA_VROWS, GQA_TQ), F32),
            pltpu.VMEM((GQA_KTILES, GQA_VROWS, GQA_TK), BF16),
        ],
        compiler_params=_cparams("parallel", "arbitrary"),
        name="gqa_attention",
    )(po, po, vt)


def kernel(x, c, ctx, c_ctx, ada_w, ada_b, ln_g, ln_b, ev_w_in, ev_w_out, na_rpb, s5_a_re, s5_a_im,
           s5_log_dt, s5_b_re, s5_b_im, s5_c_re, s5_c_im, s5_d, s5_glu_w, s5_glu_b, od_w_in,
           od_w_out, q_norm_g, k_norm_g):
    assert x.shape == (1, SEQ, D_MODEL) and ctx.shape == (1, CTX, D_MODEL)
    xa = jnp.concatenate([x[0], ctx[0]], axis=0)
    mod = _ada_mod(c, c_ctx, ada_w, ada_b)
    cos2, sin2 = _rope_tables()
    ev_w_in_b = ev_w_in.astype(BF16)
    ev_w_out_b = ev_w_out.astype(BF16)
    glu_w_b = s5_glu_w.astype(BF16)
    od_w_in_b = od_w_in.astype(BF16)
    od_w_out_b = od_w_out.astype(BF16)

    for l in range(DEPTH):
        i = l // 2
        sh = mod[l, :, :D_MODEL]
        sc = mod[l, :, D_MODEL:2 * D_MODEL]
        gt = mod[l, :, 2 * D_MODEL:]
        if l % 2 == 0:
            pe = _proj_even(xa, sc, sh, ev_w_in_b, i)
            ya = _na_attention(pe, _na_bias_table(na_rpb[i]))
            bmat, cmat, avec = _s5_tables(s5_a_re[i], s5_a_im[i], s5_log_dt[i], s5_b_re[i],
                                          s5_b_im[i], s5_c_re[i], s5_c_im[i])
            y_fwd, y_bwd = _s5_scan(pe, bmat, cmat, avec)
            xa = _out_even(xa, ya, pe, y_fwd, y_bwd, s5_d[i], glu_w_b, s5_glu_b[i], ev_w_out_b, i,
                           gt, ln_g[l], ln_b[l])
        else:
            last = l == DEPTH - 1
            gains = jnp.stack([q_norm_g[i] * (ATTN_SCALE * LOG2E), k_norm_g[i]]).reshape(2, 1, HEAD_DIM)
            po = _proj_odd(xa, sc, sh, od_w_in_b, i, gains, cos2, sin2)
            oa = _gqa_attention(po, need_ctx=not last)
            xa = _out_odd(xa, oa, po, od_w_out_b, i, gt, ln_g[l], ln_b[l], last)
    return xa.reshape(1, SEQ, D_MODEL)
```

```python
import functools
import math

import numpy as np
import jax
import jax.numpy as jnp
from jax import lax
from jax.experimental import pallas as pl
from jax.experimental.pallas import tpu as pltpu

F32 = jnp.float32
BF16 = jnp.bfloat16

D_MODEL = 2048
SEQ = 8192
CTX = 256
NTOK = SEQ + CTX
DEPTH = 4
GRID_W = 64
ROWS = SEQ // GRID_W
HEAD_DIM = 128
NA_HEADS = 8
NA_WIDTH = NA_HEADS * HEAD_DIM
NA_KH = 8
NA_KW = 16
S5_WIDTH = 1024
S5_GROUP = 16
S5_GROUPS = S5_WIDTH // S5_GROUP
S5_STATE = 64
GQA_HEADS = 16
GQA_KV_HEADS = 4
GQA_WIDTH = GQA_HEADS * HEAD_DIM
GQA_KV_WIDTH = GQA_KV_HEADS * HEAD_DIM
ROPE_THETA = 10000.0
EVEN_IN = 4 * NA_WIDTH + 2 * S5_WIDTH
ODD_IN = 2 * GQA_WIDTH + 2 * GQA_KV_WIDTH
DN_ALPHA = (2 * DEPTH) ** 0.25
LN_EPS = 1e-6
RMS_EPS = 1e-6
ATTN_SCALE = HEAD_DIM ** -0.5
LOG2E = math.log2(math.e)
MASK_VALUE = -1e30

LANES = 128
SUBLANES = 8
VMEM_LIMIT_BYTES = 56 * 1024 * 1024

PROJ_TM = 1056
PROJ_TN_EVEN = 1024
PROJ_TN = 512
OUT_TM = 384
OUT_TM_LAST = 512
NA_ROWS_PER_BLOCK = 4
NA_TQ = NA_ROWS_PER_BLOCK * GRID_W
NA_KEY_ROWS = 12
NA_TK = NA_KEY_ROWS * GRID_W
S5_T = 256
S5_SLABS = 4
S5_SLAB_CH = S5_WIDTH // S5_SLABS
S5_GS = 8
GQA_TQ = 256
GQA_TK = 256


assert NA_WIDTH % PROJ_TN_EVEN == 0 and GQA_KV_WIDTH % PROJ_TN == 0


def _cparams(*sem):
    return pltpu.CompilerParams(dimension_semantics=sem, vmem_limit_bytes=VMEM_LIMIT_BYTES)


def _dot(a, b):
    return jnp.dot(a, b, preferred_element_type=F32)


def _silu(x):
    return x * jax.nn.sigmoid(x)


ADA_TN = 1024


def _ada_kernel(c_ref, w_ref, b_ref, o_ref):
    s0 = _silu(c_ref[0])
    s1 = _silu(c_ref[1])
    for ch in range(ADA_TN // LANES):
        sl = slice(ch * LANES, (ch + 1) * LANES)
        w = w_ref[0, :, sl]
        b = b_ref[0, :, sl]
        o_ref[0, 0:1, sl] = jnp.sum(w * s0, axis=0, keepdims=True) + b
        o_ref[0, 1:2, sl] = jnp.sum(w * s1, axis=0, keepdims=True) + b


def _ada_mod(c, c_ctx, ada_w, ada_b):
    cb = jnp.broadcast_to(jnp.stack([c[0], c_ctx])[:, :, None], (2, D_MODEL, LANES))
    n = 3 * D_MODEL
    return pl.pallas_call(
        _ada_kernel,
        grid=(DEPTH, n // ADA_TN),
        in_specs=[
            pl.BlockSpec((2, D_MODEL, LANES), lambda l, j: (0, 0, 0)),
            pl.BlockSpec((1, D_MODEL, ADA_TN), lambda l, j: (l, 0, j)),
            pl.BlockSpec((1, 1, ADA_TN), lambda l, j: (l, 0, j)),
        ],
        out_specs=pl.BlockSpec((1, 2, ADA_TN), lambda l, j: (l, 0, j)),
        out_shape=jax.ShapeDtypeStruct((DEPTH, 2, n), F32),
        compiler_params=_cparams("parallel", "parallel"),
        name="ada_mod",
    )(cb, ada_w, ada_b.reshape(DEPTH, 1, n))


def _row_is_latent(i, tm):
    rows = i * tm + lax.broadcasted_iota(jnp.int32, (tm, 1), 0)
    return rows < SEQ


def _modulate(x_ref, sc_ref, sh_ref, h_scr):
    lat = _row_is_latent(pl.program_id(0), PROJ_TM)
    sc = jnp.where(lat, sc_ref[0:1, :], sc_ref[1:2, :])
    sh = jnp.where(lat, sh_ref[0:1, :], sh_ref[1:2, :])
    h_scr[...] = (x_ref[...] * (1.0 + sc) + sh).astype(BF16)


def _proj_even_kernel(x_ref, sc_ref, sh_ref, w_ref, o_ref, h_scr):
    @pl.when(pl.program_id(1) == 0)
    def _():
        _modulate(x_ref, sc_ref, sh_ref, h_scr)

    qscale = jnp.where(pl.program_id(1) < NA_WIDTH // PROJ_TN_EVEN, ATTN_SCALE * LOG2E, 1.0)
    o_ref[...] = (_dot(h_scr[...], w_ref[...]) * qscale).astype(o_ref.dtype)


GQA_Q_TILES = GQA_WIDTH // PROJ_TN
GQA_QK_TILES = (GQA_WIDTH + GQA_KV_WIDTH) // PROJ_TN


def _proj_odd_kernel(x_ref, sc_ref, sh_ref, w_ref, gain_ref, cos_ref, sin_ref, o_ref, h_scr):
    j = pl.program_id(1)

    @pl.when(j == 0)
    def _():
        _modulate(x_ref, sc_ref, sh_ref, h_scr)

    acc = _dot(h_scr[...], w_ref[...])

    @pl.when(j < GQA_QK_TILES)
    def _():
        gain = gain_ref[0]
        cos = cos_ref[...]
        sin = sin_ref[...]
        for hh in range(PROJ_TN // HEAD_DIM):
            sl = slice(hh * HEAD_DIM, (hh + 1) * HEAD_DIM)
            z = acc[:, sl]
            z = z * lax.rsqrt(jnp.mean(z * z, axis=-1, keepdims=True) + RMS_EPS) * gain
            z = z * cos + pltpu.roll(z, HEAD_DIM // 2, axis=1) * sin
            o_ref[:, sl] = z.astype(o_ref.dtype)

    @pl.when(j >= GQA_QK_TILES)
    def _():
        o_ref[...] = acc.astype(o_ref.dtype)


def _proj_common_specs(layer, tn):
    return [
        pl.BlockSpec((PROJ_TM, D_MODEL), lambda i, j: (i, 0)),
        pl.BlockSpec((2, D_MODEL), lambda i, j: (0, 0)),
        pl.BlockSpec((2, D_MODEL), lambda i, j: (0, 0)),
        pl.BlockSpec((None, D_MODEL, tn), lambda i, j: (layer, 0, j)),
    ]


def _proj_even(xa, sc, sh, w_bf16, layer):
    return pl.pallas_call(
        _proj_even_kernel,
        grid=(NTOK // PROJ_TM, EVEN_IN // PROJ_TN_EVEN),
        in_specs=_proj_common_specs(layer, PROJ_TN_EVEN),
        out_specs=pl.BlockSpec((PROJ_TM, PROJ_TN_EVEN), lambda i, j: (i, j)),
        out_shape=jax.ShapeDtypeStruct((NTOK, EVEN_IN), BF16),
        scratch_shapes=[pltpu.VMEM((PROJ_TM, D_MODEL), BF16)],
        compiler_params=_cparams("parallel", "arbitrary"),
        name="proj_even",
    )(xa, sc, sh, w_bf16)


def _proj_odd(xa, sc, sh, w_bf16, layer, gains, cos2, sin2):
    return pl.pallas_call(
        _proj_odd_kernel,
        grid=(NTOK // PROJ_TM, ODD_IN // PROJ_TN),
        in_specs=_proj_common_specs(layer, PROJ_TN) + [
            pl.BlockSpec((1, 1, HEAD_DIM), lambda i, j: (jnp.minimum(j // GQA_Q_TILES, 1), 0, 0)),
            pl.BlockSpec((PROJ_TM, HEAD_DIM), lambda i, j: (i, 0)),
            pl.BlockSpec((PROJ_TM, HEAD_DIM), lambda i, j: (i, 0)),
        ],
        out_specs=pl.BlockSpec((PROJ_TM, PROJ_TN), lambda i, j: (i, j)),
        out_shape=jax.ShapeDtypeStruct((NTOK, ODD_IN), BF16),
        scratch_shapes=[pltpu.VMEM((PROJ_TM, D_MODEL), BF16)],
        compiler_params=_cparams("parallel", "arbitrary"),
        name="proj_odd",
    )(xa, sc, sh, w_bf16, gains, cos2, sin2)


def _rope_tables():
    t = np.arange(SEQ)
    half = HEAD_DIM // 2
    per_axis = half // 2
    inv = ROPE_THETA ** (-np.arange(per_axis, dtype=np.float64) / per_axis)
    ang = np.concatenate([(t // GRID_W)[:, None] * inv, (t % GRID_W)[:, None] * inv], -1)
    cos = np.concatenate([np.cos(ang), np.ones((CTX, half))], 0)
    sin = np.concatenate([np.sin(ang), np.zeros((CTX, half))], 0)
    cos2 = np.concatenate([cos, cos], -1)
    sin2 = np.concatenate([-sin, sin], -1)
    return jnp.asarray(cos2, F32), jnp.asarray(sin2, F32)


NA_QBLOCKS = ROWS // NA_ROWS_PER_BLOCK
NA_LAST_KEY_ROW0 = ROWS - NA_KEY_ROWS
NA_DR = 2 * NA_KH - 1
NA_QPAIRS = NA_ROWS_PER_BLOCK // 2
NA_VT_TILE = 256
NA_VT_TILES = NTOK // NA_VT_TILE
NA_SLAB_TILES = NA_TK // NA_VT_TILE
NA_CTX_TILE = SEQ // NA_VT_TILE
NA_PAIRS_PER_TRIP = 8
NA_ONES_ROWS = 16
NA_VROWS = HEAD_DIM + NA_ONES_ROWS
NA_TAB_BOTH = 0
NA_TAB_FIRST = NA_TAB_BOTH + NA_DR - 1
NA_TAB_SECOND = NA_TAB_FIRST + NA_DR
NA_TAB_NONE = NA_TAB_SECOND + NA_DR
NA_TAB = NA_TAB_NONE + 1


def _na_col_onehot():
    c = np.arange(GRID_W)
    c_start = np.clip(c - NA_KW // 2, 0, GRID_W - NA_KW)
    inside = (c[None, :] >= c_start[:, None]) & (c[None, :] < c_start[:, None] + NA_KW)
    dc = c[None, :] - c[:, None] + NA_KW - 1
    onehot = (dc[None] == np.arange(2 * NA_KW - 1)[:, None, None]) & inside[None]
    return onehot.astype(np.float32), inside


_NA_ONEHOT, _NA_INSIDE = _na_col_onehot()


def _na_bias_table(rpb):
    t = jnp.einsum('hak,kcz->hazc', rpb.astype(F32) * LOG2E, _NA_ONEHOT, precision=lax.Precision.HIGHEST)
    t = jnp.where(_NA_INSIDE.T, t, MASK_VALUE)
    masked = jnp.full_like(t, MASK_VALUE)
    both = jnp.concatenate([t[:, 1:], t[:, :-1]], axis=-1)
    first = jnp.concatenate([t, masked], axis=-1)
    second = jnp.concatenate([masked, t], axis=-1)
    none = jnp.concatenate([masked[:, :1], masked[:, :1]], axis=-1)
    return jnp.concatenate([both, first, second, none], axis=1)


def _na_tile_index():
    idx = np.zeros((NA_QBLOCKS, NA_KEY_ROWS, NA_QPAIRS), np.int32)
    for qb in range(NA_QBLOCKS):
        kr0 = int(np.clip(NA_ROWS_PER_BLOCK * qb - NA_KH // 2, 0, NA_LAST_KEY_ROW0))
        for b in range(NA_KEY_ROWS):
            kr = kr0 + b
            for ip in range(NA_QPAIRS):
                r = NA_ROWS_PER_BLOCK * qb + 2 * ip
                lo0 = int(np.clip(r - NA_KH // 2, 0, ROWS - NA_KH))
                lo1 = int(np.clip(r + 1 - NA_KH // 2, 0, ROWS - NA_KH))
                in0 = lo0 <= kr < lo0 + NA_KH
                in1 = lo1 <= kr < lo1 + NA_KH
                dr1 = kr - (r + 1) + NA_KH - 1
                if in0 and in1:
                    idx[qb, b, ip] = NA_TAB_BOTH + dr1
                elif in0:
                    idx[qb, b, ip] = NA_TAB_FIRST + dr1 + 1
                elif in1:
                    idx[qb, b, ip] = NA_TAB_SECOND + dr1
                else:
                    idx[qb, b, ip] = NA_TAB_NONE
    return idx.reshape(-1)


_NA_TILE_INDEX = _na_tile_index()


def _na_bias_panel(idx_ref, tab_ref, qb):
    base = qb * (NA_KEY_ROWS * NA_QPAIRS)
    rows = []
    for b in range(NA_KEY_ROWS):
        tiles = [tab_ref[0, idx_ref[base + b * NA_QPAIRS + ip]] for ip in range(NA_QPAIRS)]
        rows.append(jnp.concatenate(tiles, axis=1))
    return jnp.concatenate(rows, axis=0)


def _na_kernel(idx_ref, q_ref, k_ref, vt_ref, tab_ref, o_ref, sa_ref, sb_ref, vte_ref):
    vte_ref[:, 0:HEAD_DIM, :] = vt_ref[0]
    vte_ref[:, HEAD_DIM:NA_VROWS, :] = jnp.ones((NA_VT_TILES, NA_ONES_ROWS, NA_VT_TILE), BF16)

    def q_transposed(qb):
        row0 = pl.multiple_of(qb * NA_TQ, NA_TQ)
        return q_ref[pl.ds(row0, NA_TQ), :].astype(F32).T.astype(BF16)

    def key_row0(qb):
        return jnp.clip(NA_ROWS_PER_BLOCK * qb - NA_KH // 2, 0, NA_LAST_KEY_ROW0)

    def scores(qb, qt, buf):
        kr0 = key_row0(qb)
        kl = k_ref[pl.ds(pl.multiple_of(kr0 * GRID_W, NA_VT_TILE), NA_TK), :]
        s_loc = _dot(kl, qt) + _na_bias_panel(idx_ref, tab_ref, qb)
        s_ctx = _dot(k_ref[pl.ds(SEQ, CTX), :], qt)
        buf[0:NA_TK] = s_loc
        buf[NA_TK:NA_TK + CTX] = s_ctx
        return jnp.maximum(jnp.max(s_loc, axis=0, keepdims=True), jnp.max(s_ctx, axis=0, keepdims=True))

    def emit(o_ext, qb):
        o = o_ext[0:HEAD_DIM] / o_ext[HEAD_DIM:HEAD_DIM + 1]
        o_ref[pl.ds(pl.multiple_of(qb * NA_TQ, NA_TQ), NA_TQ), :] = o.T.astype(o_ref.dtype)

    def attend(qb, buf, mx):
        tile0 = key_row0(qb) * GRID_W // NA_VT_TILE
        o_ext = _dot(vte_ref[NA_CTX_TILE], jnp.exp2(buf[NA_TK:NA_TK + CTX] - mx).astype(BF16))
        for j in range(NA_SLAB_TILES):
            p = jnp.exp2(buf[j * NA_VT_TILE:(j + 1) * NA_VT_TILE] - mx).astype(BF16)
            o_ext = o_ext + _dot(vte_ref[tile0 + j], p)
        return o_ext

    def pair(j, carry):
        mx, qt_odd, o_late = carry
        emit(o_late, jnp.maximum(2 * j - 1, 0))
        qt_even = q_transposed(jnp.minimum(2 * j + 2, NA_QBLOCKS - 1))
        mx_b = scores(2 * j + 1, qt_odd, sb_ref)
        emit(attend(2 * j, sa_ref, mx), 2 * j)
        qt_odd_next = q_transposed(jnp.minimum(2 * j + 3, NA_QBLOCKS))
        mx_a = scores(jnp.minimum(2 * j + 2, NA_QBLOCKS - 1), qt_even, sa_ref)
        return mx_a, qt_odd_next, attend(2 * j + 1, sb_ref, mx_b)

    def trip(t, carry):
        for u in range(NA_PAIRS_PER_TRIP):
            carry = pair(t * NA_PAIRS_PER_TRIP + u, carry)
        return carry

    first = (scores(0, q_transposed(0), sa_ref), q_transposed(1), jnp.ones((NA_VROWS, NA_TQ), F32))
    _, qt_ctx, o_late = lax.fori_loop(0, NA_QBLOCKS // (2 * NA_PAIRS_PER_TRIP), trip, first)
    emit(o_late, NA_QBLOCKS - 1)

    s_ctx = _dot(k_ref[pl.ds(SEQ, CTX), :], qt_ctx)
    p_ctx = jnp.exp2(s_ctx - jnp.max(s_ctx, axis=0, keepdims=True)).astype(BF16)
    emit(_dot(vte_ref[NA_CTX_TILE], p_ctx), NA_QBLOCKS)


def _na_attention(pe, table):
    kcol = NA_WIDTH // HEAD_DIM
    vt = pe[:, 2 * NA_WIDTH:3 * NA_WIDTH].reshape(NA_VT_TILES, NA_VT_TILE, NA_HEADS, HEAD_DIM).transpose(2, 0, 3, 1)
    return pl.pallas_call(
        _na_kernel,
        grid=(NA_HEADS,),
        in_specs=[
            pl.BlockSpec(memory_space=pltpu.SMEM),
            pl.BlockSpec((NTOK, HEAD_DIM), lambda h: (0, h)),
            pl.BlockSpec((NTOK, HEAD_DIM), lambda h: (0, kcol + h)),
            pl.BlockSpec((1, NA_VT_TILES, HEAD_DIM, NA_VT_TILE), lambda h: (h, 0, 0, 0)),
            pl.BlockSpec((1, NA_TAB, GRID_W, 2 * GRID_W), lambda h: (h, 0, 0, 0)),
        ],
        out_specs=pl.BlockSpec((NTOK, HEAD_DIM), lambda h: (0, h)),
        out_shape=jax.ShapeDtypeStruct((NTOK, NA_WIDTH), BF16),
        scratch_shapes=[
            pltpu.VMEM((NA_TK + CTX, NA_TQ), F32),
            pltpu.VMEM((NA_TK + CTX, NA_TQ), F32),
            pltpu.VMEM((NA_VT_TILES, NA_VROWS, NA_VT_TILE), BF16),
        ],
        compiler_params=_cparams("parallel"),
        name="na_attention",
    )(jnp.asarray(_NA_TILE_INDEX), pe, pe, vt, table)


S5_NCHUNK = NTOK // S5_T
S5_LAT_CHUNKS = SEQ // S5_T
S5_DROWS = S5_T * S5_GS
S5_PAIR_CH = 2 * S5_GROUP
S5_PIECES = 8
S5_PIECE_T = S5_T // S5_PIECES
S5_RE_IM_LANES = S5_GS * LANES
S5_HALF_T = S5_T // 2
S5_HALF_ROWS = S5_DROWS // 2


def _s5_fwd_chunk(j):
    return (j + S5_LAT_CHUNKS) % S5_NCHUNK


def _s5_bwd_chunk(j):
    return S5_NCHUNK - 1 - j


def _s5_kernel(uf_ref, ub_ref, bc_ref, c_ref, a_ref, yf_out, yb_out,
               b_ref, xrf_ref, xif_ref, xrb_ref, xib_ref, yfull_ref, st_ref):
    dirs = ((uf_ref, xrf_ref, xif_ref, yf_out), (ub_ref, xrb_ref, xib_ref, yb_out))

    @pl.when(pl.program_id(0) == 0)
    def _():
        st_ref[...] = jnp.zeros_like(st_ref)
        b_ref[...] = jnp.zeros_like(b_ref)
        for d in range(2):
            for s in range(S5_SLABS):
                for gs in range(S5_GS):
                    rows = slice(gs * S5_PAIR_CH, (gs + 1) * S5_PAIR_CH)
                    b_ref[d, s, rows, gs * LANES:(gs + 1) * LANES] = bc_ref[d, s, rows, 0:LANES]
                    b_ref[d, s, rows, (S5_GS + gs) * LANES:(S5_GS + gs + 1) * LANES] = (
                        bc_ref[d, s, rows, LANES:2 * LANES])

    def drive_piece(d, k):
        u_ref, xr_ref, xi_ref, _ = dirs[d]
        s, half = divmod(k, 2)
        x = _dot(u_ref[:, s * S5_SLAB_CH:(s + 1) * S5_SLAB_CH],
                 b_ref[d, s, :, half * S5_RE_IM_LANES:(half + 1) * S5_RE_IM_LANES])
        dst = xi_ref if half else xr_ref
        for gs in range(S5_GS):
            dst[s, pl.ds(gs, S5_T, stride=S5_GS), :] = x[:, gs * LANES:(gs + 1) * LANES]

    def recurrence_piece(d, k, carry):
        _, xr_ref, xi_ref, _ = dirs[d]
        ar = a_ref[d, 0]
        ai = a_ref[d, 1]
        hr, hi = carry
        for n in range(S5_PIECE_T):
            t = k * S5_PIECE_T + n
            if d == 1:
                t = S5_T - 1 - t
            rows = slice(t * S5_GS, (t + 1) * S5_GS)
            hr, hi = (ar * hr - ai * hi + xr_ref[:, rows, :],
                      ar * hi + ai * hr + xi_ref[:, rows, :])
            xr_ref[:, rows, :] = hr
            xi_ref[:, rows, :] = hi
        return hr, hi

    pairs_per_half = LANES // S5_PAIR_CH
    lane_pair = lax.broadcasted_iota(jnp.int32, (S5_HALF_T, LANES), 1) // S5_PAIR_CH

    def readout_piece(d, k):
        _, xr_ref, xi_ref, y_ref = dirs[d]
        s, th = divmod(k, 2)
        rows = slice(th * S5_HALF_ROWS, (th + 1) * S5_HALF_ROWS)
        hcat = jnp.concatenate([xr_ref[s, rows, :], xi_ref[s, rows, :]], axis=1).astype(BF16)
        yfull = _dot(hcat, c_ref[d, s])
        for hf in range(2):
            yfull_ref[d, hf] = yfull[:, hf * LANES:(hf + 1) * LANES]
        for hf in range(2):
            acc = jnp.zeros((S5_HALF_T, LANES), F32)
            for gp in range(pairs_per_half):
                gs = hf * pairs_per_half + gp
                acc = jnp.where(lane_pair == gp, yfull_ref[d, hf, pl.ds(gs, S5_HALF_T, stride=S5_GS), :], acc)
            col = s * S5_SLAB_CH + hf * LANES
            y_ref[th * S5_HALF_T:(th + 1) * S5_HALF_T, col:col + LANES] = acc

    for k in range(S5_PIECES):
        drive_piece(0, k)
    carry = (st_ref[0, 0], st_ref[0, 1])
    for k in range(S5_PIECES):
        carry = recurrence_piece(0, k, carry)
        drive_piece(1, k)
    st_ref[0, 0], st_ref[0, 1] = carry
    carry = (st_ref[1, 0], st_ref[1, 1])
    for k in range(S5_PIECES):
        carry = recurrence_piece(1, k, carry)
        readout_piece(0, k)
    st_ref[1, 0], st_ref[1, 1] = carry
    for k in range(S5_PIECES):
        readout_piece(1, k)


def _s5_tables(a_re, a_im, log_dt, b_re, b_im, c_re, c_im):
    a_re = a_re.astype(F32); a_im = a_im.astype(F32)
    dt = jnp.exp(log_dt.astype(F32))[..., None]
    mag = jnp.exp(a_re * dt)
    lb_re = mag * jnp.cos(a_im * dt)
    lb_im = mag * jnp.sin(a_im * dt)
    den = a_re * a_re + a_im * a_im
    n_re, n_im = lb_re - 1.0, lb_im
    f_re = (n_re * a_re + n_im * a_im) / den
    f_im = (n_im * a_re - n_re * a_im) / den
    bb_re = f_re[..., None] * b_re - f_im[..., None] * b_im
    bb_im = f_re[..., None] * b_im + f_im[..., None] * b_re

    def split(z):
        return z.reshape((2, S5_SLABS, S5_GS, 2) + z.shape[2:])

    eye_gl = jnp.eye(2, dtype=F32)

    def drive(bb):
        z = split(bb)
        z = jnp.einsum('dsglpc,lm->dsglcmp', z, eye_gl)
        return z.reshape(2, S5_SLABS, S5_SLAB_CH, LANES)

    bmat = jnp.concatenate([drive(bb_re), drive(bb_im)], axis=-1).astype(BF16)

    def readout(cc):
        z = split(cc.astype(F32))
        z = jnp.einsum('dshmcp,lm->dslphmc', z, eye_gl)
        return z.reshape(2, S5_SLABS, LANES, S5_SLAB_CH)

    cmat = jnp.concatenate([readout(c_re), -readout(c_im)], axis=2).astype(BF16)

    def dense(z):
        return split(z).reshape(2, S5_SLABS, S5_GS, LANES)

    avec = jnp.stack([dense(lb_re), dense(lb_im)], axis=1)
    return bmat, cmat, avec


def _s5_scan(pe, bmat, cmat, avec):
    ucol = 4 * NA_WIDTH // S5_WIDTH

    def whole(arr):
        nd = arr.ndim
        return pl.BlockSpec(arr.shape, lambda j: (0,) * nd, pipeline_mode=pl.Buffered(1))

    dense = pltpu.VMEM((S5_SLABS, S5_DROWS, LANES), F32)
    return pl.pallas_call(
        _s5_kernel,
        grid=(S5_NCHUNK,),
        in_specs=[
            pl.BlockSpec((S5_T, S5_WIDTH), lambda j: (_s5_fwd_chunk(j), ucol)),
            pl.BlockSpec((S5_T, S5_WIDTH), lambda j: (_s5_bwd_chunk(j), ucol)),
            whole(bmat), whole(cmat), whole(avec),
        ],
        out_specs=[
            pl.BlockSpec((S5_T, S5_WIDTH), lambda j: (_s5_fwd_chunk(j), 0)),
            pl.BlockSpec((S5_T, S5_WIDTH), lambda j: (_s5_bwd_chunk(j), 0)),
        ],
        out_shape=[jax.ShapeDtypeStruct((NTOK, S5_WIDTH), F32)] * 2,
        scratch_shapes=[
            pltpu.VMEM((2, S5_SLABS, S5_SLAB_CH, 2 * S5_RE_IM_LANES), BF16),
            dense, dense, dense, dense,
            pltpu.VMEM((2, 2, S5_HALF_ROWS, LANES), F32),
            pltpu.VMEM((2, 2, S5_SLABS, S5_GS, LANES), F32),
        ],
        compiler_params=_cparams("arbitrary"),
        name="s5_scan",
    )(pe, pe, bmat, cmat, avec)


def _residual_layer_norm(x, y, gt_ref, lng_ref, lnb_ref, lat):
    gt = jnp.where(lat, gt_ref[0:1, :], gt_ref[1:2, :])
    r = DN_ALPHA * x + gt * y
    mu = jnp.mean(r, axis=-1, keepdims=True)
    rc = r - mu
    var = jnp.mean(rc * rc, axis=-1, keepdims=True)
    return rc * lax.rsqrt(var + LN_EPS) * lng_ref[...] + lnb_ref[...]


def _out_even_kernel(tm, x_ref, ya_ref, ga_ref, u_ref, gb_ref, yf_ref, yb_ref, sd_ref,
                     gw_ref, gbias_ref, wo_ref, gt_ref, lng_ref, lnb_ref, o_ref):
    lat = _row_is_latent(pl.program_id(0), tm)
    za = ya_ref[...].astype(F32) * _silu(ga_ref[...].astype(F32))
    yb = u_ref[...].astype(F32) * sd_ref[...] + yf_ref[...] + yb_ref[...]
    zb = jax.nn.gelu(yb, approximate=True)
    zb = zb * jax.nn.sigmoid(_dot(zb.astype(BF16), gw_ref[...]) + gbias_ref[...])
    zb = zb * _silu(gb_ref[...].astype(F32))
    cat = jnp.concatenate([za.astype(BF16), zb.astype(BF16)], axis=1)
    y = _dot(cat, wo_ref[...])
    o_ref[...] = _residual_layer_norm(x_ref[...], y, gt_ref, lng_ref, lnb_ref, lat)


def _out_odd_kernel(tm, x_ref, o0_ref, o1_ref, g0_ref, g1_ref, wo_ref, gt_ref, lng_ref, lnb_ref, o_ref):
    lat = _row_is_latent(pl.program_id(0), tm)
    z0 = o0_ref[...].astype(F32) * _silu(g0_ref[...].astype(F32))
    z1 = o1_ref[...].astype(F32) * _silu(g1_ref[...].astype(F32))
    cat = jnp.concatenate([z0.astype(BF16), z1.astype(BF16)], axis=1)
    y = _dot(cat, wo_ref[...])
    o_ref[...] = _residual_layer_norm(x_ref[...], y, gt_ref, lng_ref, lnb_ref, lat)


def _const_spec(shape, layer=None):
    nd = len(shape)
    if layer is None:
        return pl.BlockSpec(shape, lambda i: (0,) * nd, pipeline_mode=pl.Buffered(1))
    return pl.BlockSpec((None,) + shape, lambda i: (layer,) + (0,) * nd, pipeline_mode=pl.Buffered(1))


def _out_even(xa, ya, pe, y_fwd, y_bwd, s5_d, glu_w, glu_b, w_out, layer, gt, ln_g, ln_b):
    tm = OUT_TM
    half = S5_WIDTH

    def col(c):
        return pl.BlockSpec((tm, half), lambda i: (i, c))

    return pl.pallas_call(
        functools.partial(_out_even_kernel, tm),
        grid=(NTOK // tm,),
        in_specs=[
            pl.BlockSpec((tm, D_MODEL), lambda i: (i, 0)),
            col(0),
            col(3), col(4), col(5),
            col(0),
            col(0),
            _const_spec((1, half)),
            _const_spec((half, half), layer),
            _const_spec((1, half)),
            _const_spec((D_MODEL, D_MODEL), layer),
            _const_spec((2, D_MODEL)),
            _const_spec((1, D_MODEL)),
            _const_spec((1, D_MODEL)),
        ],
        out_specs=pl.BlockSpec((tm, D_MODEL), lambda i: (i, 0)),
        out_shape=jax.ShapeDtypeStruct((NTOK, D_MODEL), F32),
        compiler_params=_cparams("parallel"),
        name="out_even",
    )(xa, ya, pe, pe, pe, y_fwd, y_bwd, s5_d.reshape(1, half), glu_w, glu_b.reshape(1, half),
      w_out, gt, ln_g.reshape(1, D_MODEL), ln_b.reshape(1, D_MODEL))


def _out_odd(xa, oa, po, w_out, layer, gt, ln_g, ln_b, last):
    tm = OUT_TM_LAST if last else OUT_TM
    nrows = SEQ if last else NTOK
    half = GQA_WIDTH // 2
    gcol = (GQA_WIDTH + 2 * GQA_KV_WIDTH) // half

    return pl.pallas_call(
        functools.partial(_out_odd_kernel, tm),
        grid=(nrows // tm,),
        in_specs=[
            pl.BlockSpec((tm, D_MODEL), lambda i: (i, 0)),
            pl.BlockSpec((tm, half), lambda i: (i, 0)),
            pl.BlockSpec((tm, half), lambda i: (i, 1)),
            pl.BlockSpec((tm, half), lambda i: (i, gcol)),
            pl.BlockSpec((tm, half), lambda i: (i, gcol + 1)),
            _const_spec((D_MODEL, D_MODEL), layer),
            _const_spec((2, D_MODEL)),
            _const_spec((1, D_MODEL)),
            _const_spec((1, D_MODEL)),
        ],
        out_specs=pl.BlockSpec((tm, D_MODEL), lambda i: (i, 0)),
        out_shape=jax.ShapeDtypeStruct((nrows, D_MODEL), F32),
        compiler_params=_cparams("parallel"),
        name="out_odd",
    )(xa, oa, oa, po, po, w_out, gt, ln_g.reshape(1, D_MODEL), ln_b.reshape(1, D_MODEL))


GQA_GROUP = GQA_HEADS // GQA_KV_HEADS
GQA_QBLOCKS = SEQ // GQA_TQ
GQA_KTILES = NTOK // GQA_TK
GQA_CTX_TILE = SEQ // GQA_TK
GQA_ONES_ROWS = 16
GQA_VROWS = HEAD_DIM + GQA_ONES_ROWS
GQA_UNROLL = 32


def _gqa_kernel(q_ref, k_ref, vt_ref, o_ref, sa_ref, sb_ref, m_ref, acc_ref, vte_ref):
    qb = pl.program_id(1)

    @pl.when(qb == 0)
    def _():
        vte_ref[:, 0:HEAD_DIM, :] = vt_ref[0]
        vte_ref[:, HEAD_DIM:GQA_VROWS, :] = jnp.ones((GQA_KTILES, GQA_ONES_ROWS, GQA_TK), BF16)

    qts = [q_ref[:, g * HEAD_DIM:(g + 1) * HEAD_DIM].astype(F32).T.astype(BF16) for g in range(GQA_GROUP)]
    is_ctx = qb == GQA_QBLOCKS

    def scores(tile, buf, g):
        k = k_ref[pl.ds(pl.multiple_of(tile * GQA_TK, GQA_TK), GQA_TK), :]
        s = _dot(k, qts[g])
        buf[g] = s
        return jnp.max(s, axis=0, keepdims=True)

    def accumulate(tile, buf, g, mx):
        m_prev = m_ref[g]
        m_cur = jnp.maximum(m_prev, mx)
        alpha = jnp.exp2(m_prev - m_cur)
        p = jnp.exp2(buf[g] - m_cur)
        acc_ref[g] = alpha * acc_ref[g] + _dot(vte_ref[tile], p.astype(BF16))
        m_ref[g] = m_cur

    def step(tile, cur, nxt, mxs):
        out = []
        for g in range(GQA_GROUP):
            out.append(scores(tile + 1, nxt, g))
            accumulate(tile, cur, g, mxs[g])
        return tuple(out)

    m_ref[...] = jnp.full_like(m_ref, MASK_VALUE)
    acc_ref[...] = jnp.zeros_like(acc_ref)
    first = jnp.where(is_ctx, GQA_CTX_TILE, 0)
    mxs = tuple(scores(first, sa_ref, g) for g in range(GQA_GROUP))

    def group(j, mxs):
        for u in range(0, GQA_UNROLL, 2):
            mxs = step(GQA_UNROLL * j + u, sa_ref, sb_ref, mxs)
            mxs = step(GQA_UNROLL * j + u + 1, sb_ref, sa_ref, mxs)
        return mxs

    mxs = lax.fori_loop(0, jnp.where(is_ctx, 0, GQA_CTX_TILE // GQA_UNROLL), group, mxs)
    for g in range(GQA_GROUP):
        accumulate(GQA_CTX_TILE, sa_ref, g, mxs[g])
        o = acc_ref[g, 0:HEAD_DIM] / acc_ref[g, HEAD_DIM:HEAD_DIM + 1]
        o_ref[:, g * HEAD_DIM:(g + 1) * HEAD_DIM] = o.T.astype(o_ref.dtype)


def _gqa_attention(po, need_ctx):
    kcol = GQA_WIDTH // HEAD_DIM
    v0 = GQA_WIDTH + GQA_KV_WIDTH
    gw = GQA_GROUP * HEAD_DIM
    nqb = GQA_QBLOCKS + (1 if need_ctx else 0)
    vt = po[:, v0:v0 + GQA_KV_WIDTH].reshape(GQA_KTILES, GQA_TK, GQA_KV_HEADS, HEAD_DIM).transpose(2, 0, 3, 1)
    return pl.pallas_call(
        _gqa_kernel,
        grid=(GQA_KV_HEADS, nqb),
        in_specs=[
            pl.BlockSpec((GQA_TQ, gw), lambda h, qb: (qb, h)),
            pl.BlockSpec((NTOK, HEAD_DIM), lambda h, qb: (0, kcol + h)),
            pl.BlockSpec((1, GQA_KTILES, HEAD_DIM, GQA_TK), lambda h, qb: (h, 0, 0, 0)),
        ],
        out_specs=pl.BlockSpec((GQA_TQ, gw), lambda h, qb: (qb, h)),
        out_shape=jax.ShapeDtypeStruct((nqb * GQA_TQ, GQA_WIDTH), BF16),
        scratch_shapes=[
            pltpu.VMEM((GQA_GROUP, GQA_TK, GQA_TQ), F32),
            pltpu.VMEM((GQA_GROUP, GQA_TK, GQA_TQ), F32),
            pltpu.VMEM((GQA_GROUP, 1, GQA_TQ), F32),
            pltpu.VMEM((GQA_GROUP, GQA_VROWS, GQA_TQ), F32),
            pltpu.VMEM((GQA_KTILES, GQA_VROWS, GQA_TK), BF16),
        ],
        compiler_params=_cparams("parallel", "arbitrary"),
        name="gqa_attention",
    )(po, po, vt)


def kernel(x, c, ctx, c_ctx, ada_w, ada_b, ln_g, ln_b, ev_w_in, ev_w_out, na_rpb, s5_a_re, s5_a_im,
           s5_log_dt, s5_b_re, s5_b_im, s5_c_re, s5_c_im, s5_d, s5_glu_w, s5_glu_b, od_w_in,
           od_w_out, q_norm_g, k_norm_g):
    assert x.shape == (1, SEQ, D_MODEL) and ctx.shape == (1, CTX, D_MODEL)
    xa = jnp.concatenate([x[0], ctx[0]], axis=0)
    mod = _ada_mod(c, c_ctx, ada_w, ada_b)
    cos2, sin2 = _rope_tables()
    ev_w_in_b = ev_w_in.astype(BF16)
    ev_w_out_b = ev_w_out.astype(BF16)
    glu_w_b = s5_glu_w.astype(BF16)
    od_w_in_b = od_w_in.astype(BF16)
    od_w_out_b = od_w_out.astype(BF16)

    for l in range(DEPTH):
        i = l // 2
        sh = mod[l, :, :D_MODEL]
        sc = mod[l, :, D_MODEL:2 * D_MODEL]
        gt = mod[l, :, 2 * D_MODEL:]
        if l % 2 == 0:
            pe = _proj_even(xa, sc, sh, ev_w_in_b, i)
            ya = _na_attention(pe, _na_bias_table(na_rpb[i]))
            bmat, cmat, avec = _s5_tables(s5_a_re[i], s5_a_im[i], s5_log_dt[i], s5_b_re[i],
                                          s5_b_im[i], s5_c_re[i], s5_c_im[i])
            y_fwd, y_bwd = _s5_scan(pe, bmat, cmat, avec)
            xa = _out_even(xa, ya, pe, y_fwd, y_bwd, s5_d[i], glu_w_b, s5_glu_b[i], ev_w_out_b, i,
                           gt, ln_g[l], ln_b[l])
        else:
            last = l == DEPTH - 1
            gains = jnp.stack([q_norm_g[i] * (ATTN_SCALE * LOG2E), k_norm_g[i]]).reshape(2, 1, HEAD_DIM)
            po = _proj_odd(xa, sc, sh, od_w_in_b, i, gains, cos2, sin2)
            oa = _gqa_attention(po, need_ctx=not last)
            xa = _out_odd(xa, oa, po, od_w_out_b, i, gt, ln_g[l], ln_b[l], last)
    return xa.reshape(1, SEQ, D_MODEL)
```

```python
import functools
import math

import numpy as np
import jax
import jax.numpy as jnp
from jax import lax
from jax.experimental import pallas as pl
from jax.experimental.pallas import tpu as pltpu

F32 = jnp.float32
BF16 = jnp.bfloat16

D_MODEL = 2048
SEQ = 8192
CTX = 256
NTOK = SEQ + CTX
DEPTH = 4
GRID_W = 64
ROWS = SEQ // GRID_W
HEAD_DIM = 128
NA_HEADS = 8
NA_WIDTH = NA_HEADS * HEAD_DIM
NA_KH = 8
NA_KW = 16
S5_WIDTH = 1024
S5_GROUP = 16
S5_GROUPS = S5_WIDTH // S5_GROUP
S5_STATE = 64
GQA_HEADS = 16
GQA_KV_HEADS = 4
GQA_WIDTH = GQA_HEADS * HEAD_DIM
GQA_KV_WIDTH = GQA_KV_HEADS * HEAD_DIM
ROPE_THETA = 10000.0
EVEN_IN = 4 * NA_WIDTH + 2 * S5_WIDTH
ODD_IN = 2 * GQA_WIDTH + 2 * GQA_KV_WIDTH
DN_ALPHA = (2 * DEPTH) ** 0.25
LN_EPS = 1e-6
RMS_EPS = 1e-6
ATTN_SCALE = HEAD_DIM ** -0.5
LOG2E = math.log2(math.e)
MASK_VALUE = -1e30

LANES = 128
SUBLANES = 8
VMEM_LIMIT_BYTES = 56 * 1024 * 1024

PROJ_TM = 1056
PROJ_TN_EVEN = 1024
PROJ_TN = 512
OUT_TM = 384
OUT_TM_LAST = 512
NA_ROWS_PER_BLOCK = 4
NA_TQ = NA_ROWS_PER_BLOCK * GRID_W
NA_KEY_ROWS = 12
NA_TK = NA_KEY_ROWS * GRID_W
S5_T = 256
S5_SLABS = 4
S5_SLAB_CH = S5_WIDTH // S5_SLABS
S5_GS = 8
GQA_TQ = 256
GQA_TK = 256


assert NA_WIDTH % PROJ_TN_EVEN == 0 and GQA_KV_WIDTH % PROJ_TN == 0


def _cparams(*sem):
    return pltpu.CompilerParams(dimension_semantics=sem, vmem_limit_bytes=VMEM_LIMIT_BYTES)


def _dot(a, b):
    return jnp.dot(a, b, preferred_element_type=F32)


def _silu(x):
    return x * jax.nn.sigmoid(x)


ADA_TN = 2048


def _ada_kernel(c_ref, w_ref, b_ref, o_ref):
    s0 = _silu(c_ref[0])
    s1 = _silu(c_ref[1])
    for ch in range(ADA_TN // LANES):
        sl = slice(ch * LANES, (ch + 1) * LANES)
        w = w_ref[0, :, sl]
        b = b_ref[0, :, sl]
        o_ref[0, 0:1, sl] = jnp.sum(w * s0, axis=0, keepdims=True) + b
        o_ref[0, 1:2, sl] = jnp.sum(w * s1, axis=0, keepdims=True) + b


def _ada_mod(c, c_ctx, ada_w, ada_b):
    cb = jnp.broadcast_to(jnp.stack([c[0], c_ctx])[:, :, None], (2, D_MODEL, LANES))
    n = 3 * D_MODEL
    return pl.pallas_call(
        _ada_kernel,
        grid=(DEPTH, n // ADA_TN),
        in_specs=[
            pl.BlockSpec((2, D_MODEL, LANES), lambda l, j: (0, 0, 0)),
            pl.BlockSpec((1, D_MODEL, ADA_TN), lambda l, j: (l, 0, j)),
            pl.BlockSpec((1, 1, ADA_TN), lambda l, j: (l, 0, j)),
        ],
        out_specs=pl.BlockSpec((1, 2, ADA_TN), lambda l, j: (l, 0, j)),
        out_shape=jax.ShapeDtypeStruct((DEPTH, 2, n), F32),
        compiler_params=_cparams("parallel", "parallel"),
        name="ada_mod",
    )(cb, ada_w, ada_b.reshape(DEPTH, 1, n))


def _row_is_latent(i, tm):
    rows = i * tm + lax.broadcasted_iota(jnp.int32, (tm, 1), 0)
    return rows < SEQ


def _modulate(x_ref, sc_ref, sh_ref, h_scr):
    lat = _row_is_latent(pl.program_id(0), PROJ_TM)
    sc = jnp.where(lat, sc_ref[0:1, :], sc_ref[1:2, :])
    sh = jnp.where(lat, sh_ref[0:1, :], sh_ref[1:2, :])
    h_scr[...] = (x_ref[...] * (1.0 + sc) + sh).astype(BF16)


def _proj_even_kernel(x_ref, sc_ref, sh_ref, w_ref, o_ref, h_scr):
    @pl.when(pl.program_id(1) == 0)
    def _():
        _modulate(x_ref, sc_ref, sh_ref, h_scr)

    qscale = jnp.where(pl.program_id(1) < NA_WIDTH // PROJ_TN_EVEN, ATTN_SCALE * LOG2E, 1.0)
    o_ref[...] = (_dot(h_scr[...], w_ref[...]) * qscale).astype(o_ref.dtype)


GQA_Q_TILES = GQA_WIDTH // PROJ_TN
GQA_QK_TILES = (GQA_WIDTH + GQA_KV_WIDTH) // PROJ_TN


def _proj_odd_kernel(x_ref, sc_ref, sh_ref, w_ref, gain_ref, cos_ref, sin_ref, o_ref, h_scr):
    j = pl.program_id(1)

    @pl.when(j == 0)
    def _():
        _modulate(x_ref, sc_ref, sh_ref, h_scr)

    acc = _dot(h_scr[...], w_ref[...])

    @pl.when(j < GQA_QK_TILES)
    def _():
        gain = gain_ref[0]
        cos = cos_ref[...]
        sin = sin_ref[...]
        for hh in range(PROJ_TN // HEAD_DIM):
            sl = slice(hh * HEAD_DIM, (hh + 1) * HEAD_DIM)
            z = acc[:, sl]
            z = z * lax.rsqrt(jnp.mean(z * z, axis=-1, keepdims=True) + RMS_EPS) * gain
            z = z * cos + pltpu.roll(z, HEAD_DIM // 2, axis=1) * sin
            o_ref[:, sl] = z.astype(o_ref.dtype)

    @pl.when(j >= GQA_QK_TILES)
    def _():
        o_ref[...] = acc.astype(o_ref.dtype)


def _proj_common_specs(layer, tn):
    return [
        pl.BlockSpec((PROJ_TM, D_MODEL), lambda i, j: (i, 0)),
        pl.BlockSpec((2, D_MODEL), lambda i, j: (0, 0)),
        pl.BlockSpec((2, D_MODEL), lambda i, j: (0, 0)),
        pl.BlockSpec((None, D_MODEL, tn), lambda i, j: (layer, 0, j)),
    ]


def _proj_even(xa, sc, sh, w_bf16, layer):
    return pl.pallas_call(
        _proj_even_kernel,
        grid=(NTOK // PROJ_TM, EVEN_IN // PROJ_TN_EVEN),
        in_specs=_proj_common_specs(layer, PROJ_TN_EVEN),
        out_specs=pl.BlockSpec((PROJ_TM, PROJ_TN_EVEN), lambda i, j: (i, j)),
        out_shape=jax.ShapeDtypeStruct((NTOK, EVEN_IN), BF16),
        scratch_shapes=[pltpu.VMEM((PROJ_TM, D_MODEL), BF16)],
        compiler_params=_cparams("parallel", "arbitrary"),
        name="proj_even",
    )(xa, sc, sh, w_bf16)


def _proj_odd(xa, sc, sh, w_bf16, layer, gains, cos2, sin2):
    return pl.pallas_call(
        _proj_odd_kernel,
        grid=(NTOK // PROJ_TM, ODD_IN // PROJ_TN),
        in_specs=_proj_common_specs(layer, PROJ_TN) + [
            pl.BlockSpec((1, 1, HEAD_DIM), lambda i, j: (jnp.minimum(j // GQA_Q_TILES, 1), 0, 0)),
            pl.BlockSpec((PROJ_TM, HEAD_DIM), lambda i, j: (i, 0)),
            pl.BlockSpec((PROJ_TM, HEAD_DIM), lambda i, j: (i, 0)),
        ],
        out_specs=pl.BlockSpec((PROJ_TM, PROJ_TN), lambda i, j: (i, j)),
        out_shape=jax.ShapeDtypeStruct((NTOK, ODD_IN), BF16),
        scratch_shapes=[pltpu.VMEM((PROJ_TM, D_MODEL), BF16)],
        compiler_params=_cparams("parallel", "arbitrary"),
        name="proj_odd",
    )(xa, sc, sh, w_bf16, gains, cos2, sin2)


def _rope_tables():
    t = np.arange(SEQ)
    half = HEAD_DIM // 2
    per_axis = half // 2
    inv = ROPE_THETA ** (-np.arange(per_axis, dtype=np.float64) / per_axis)
    ang = np.concatenate([(t // GRID_W)[:, None] * inv, (t % GRID_W)[:, None] * inv], -1)
    cos = np.concatenate([np.cos(ang), np.ones((CTX, half))], 0)
    sin = np.concatenate([np.sin(ang), np.zeros((CTX, half))], 0)
    cos2 = np.concatenate([cos, cos], -1)
    sin2 = np.concatenate([-sin, sin], -1)
    return jnp.asarray(cos2, F32), jnp.asarray(sin2, F32)


NA_QBLOCKS = ROWS // NA_ROWS_PER_BLOCK
NA_LAST_KEY_ROW0 = ROWS - NA_KEY_ROWS
NA_DR = 2 * NA_KH - 1
NA_QPAIRS = NA_ROWS_PER_BLOCK // 2
NA_VT_TILE = 256
NA_VT_TILES = NTOK // NA_VT_TILE
NA_SLAB_TILES = NA_TK // NA_VT_TILE
NA_CTX_TILE = SEQ // NA_VT_TILE
NA_PAIRS_PER_TRIP = 8
NA_ONES_ROWS = 16
NA_VROWS = HEAD_DIM + NA_ONES_ROWS
NA_TAB_BOTH = 0
NA_TAB_FIRST = NA_TAB_BOTH + NA_DR - 1
NA_TAB_SECOND = NA_TAB_FIRST + NA_DR
NA_TAB_NONE = NA_TAB_SECOND + NA_DR
NA_TAB = NA_TAB_NONE + 1


def _na_col_onehot():
    c = np.arange(GRID_W)
    c_start = np.clip(c - NA_KW // 2, 0, GRID_W - NA_KW)
    inside = (c[None, :] >= c_start[:, None]) & (c[None, :] < c_start[:, None] + NA_KW)
    dc = c[None, :] - c[:, None] + NA_KW - 1
    onehot = (dc[None] == np.arange(2 * NA_KW - 1)[:, None, None]) & inside[None]
    return onehot.astype(np.float32), inside


_NA_ONEHOT, _NA_INSIDE = _na_col_onehot()


def _na_bias_table(rpb):
    t = jnp.einsum('hak,kcz->hazc', rpb.astype(F32) * LOG2E, _NA_ONEHOT, precision=lax.Precision.HIGHEST)
    t = jnp.where(_NA_INSIDE.T, t, MASK_VALUE)
    masked = jnp.full_like(t, MASK_VALUE)
    both = jnp.concatenate([t[:, 1:], t[:, :-1]], axis=-1)
    first = jnp.concatenate([t, masked], axis=-1)
    second = jnp.concatenate([masked, t], axis=-1)
    none = jnp.concatenate([masked[:, :1], masked[:, :1]], axis=-1)
    return jnp.concatenate([both, first, second, none], axis=1)


def _na_tile_index():
    idx = np.zeros((NA_QBLOCKS, NA_KEY_ROWS, NA_QPAIRS), np.int32)
    for qb in range(NA_QBLOCKS):
        kr0 = int(np.clip(NA_ROWS_PER_BLOCK * qb - NA_KH // 2, 0, NA_LAST_KEY_ROW0))
        for b in range(NA_KEY_ROWS):
            kr = kr0 + b
            for ip in range(NA_QPAIRS):
                r = NA_ROWS_PER_BLOCK * qb + 2 * ip
                lo0 = int(np.clip(r - NA_KH // 2, 0, ROWS - NA_KH))
                lo1 = int(np.clip(r + 1 - NA_KH // 2, 0, ROWS - NA_KH))
                in0 = lo0 <= kr < lo0 + NA_KH
                in1 = lo1 <= kr < lo1 + NA_KH
                dr1 = kr - (r + 1) + NA_KH - 1
                if in0 and in1:
                    idx[qb, b, ip] = NA_TAB_BOTH + dr1
                elif in0:
                    idx[qb, b, ip] = NA_TAB_FIRST + dr1 + 1
                elif in1:
                    idx[qb, b, ip] = NA_TAB_SECOND + dr1
                else:
                    idx[qb, b, ip] = NA_TAB_NONE
    return idx.reshape(-1)


_NA_TILE_INDEX = _na_tile_index()


def _na_bias_panel(idx_ref, tab_ref, qb):
    base = qb * (NA_KEY_ROWS * NA_QPAIRS)
    rows = []
    for b in range(NA_KEY_ROWS):
        tiles = [tab_ref[0, idx_ref[base + b * NA_QPAIRS + ip]] for ip in range(NA_QPAIRS)]
        rows.append(jnp.concatenate(tiles, axis=1))
    return jnp.concatenate(rows, axis=0)


def _na_kernel(idx_ref, q_ref, k_ref, vt_ref, tab_ref, o_ref, sa_ref, sb_ref, vte_ref):
    vte_ref[:, 0:HEAD_DIM, :] = vt_ref[0]
    vte_ref[:, HEAD_DIM:NA_VROWS, :] = jnp.ones((NA_VT_TILES, NA_ONES_ROWS, NA_VT_TILE), BF16)

    def q_transposed(qb):
        row0 = pl.multiple_of(qb * NA_TQ, NA_TQ)
        return q_ref[pl.ds(row0, NA_TQ), :].astype(F32).T.astype(BF16)

    def key_row0(qb):
        return jnp.clip(NA_ROWS_PER_BLOCK * qb - NA_KH // 2, 0, NA_LAST_KEY_ROW0)

    def scores(qb, qt, buf):
        kr0 = key_row0(qb)
        kl = k_ref[pl.ds(pl.multiple_of(kr0 * GRID_W, NA_VT_TILE), NA_TK), :]
        s_loc = _dot(kl, qt) + _na_bias_panel(idx_ref, tab_ref, qb)
        s_ctx = _dot(k_ref[pl.ds(SEQ, CTX), :], qt)
        buf[0:NA_TK] = s_loc
        buf[NA_TK:NA_TK + CTX] = s_ctx
        return jnp.maximum(jnp.max(s_loc, axis=0, keepdims=True), jnp.max(s_ctx, axis=0, keepdims=True))

    def emit(o_ext, qb):
        o = o_ext[0:HEAD_DIM] / o_ext[HEAD_DIM:HEAD_DIM + 1]
        o_ref[pl.ds(pl.multiple_of(qb * NA_TQ, NA_TQ), NA_TQ), :] = o.T.astype(o_ref.dtype)

    def attend(qb, buf, mx):
        tile0 = key_row0(qb) * GRID_W // NA_VT_TILE
        o_ext = _dot(vte_ref[NA_CTX_TILE], jnp.exp2(buf[NA_TK:NA_TK + CTX] - mx).astype(BF16))
        for j in range(NA_SLAB_TILES):
            p = jnp.exp2(buf[j * NA_VT_TILE:(j + 1) * NA_VT_TILE] - mx).astype(BF16)
            o_ext = o_ext + _dot(vte_ref[tile0 + j], p)
        return o_ext

    def pair(j, carry):
        mx, qt_odd, o_late = carry
        emit(o_late, jnp.maximum(2 * j - 1, 0))
        qt_even = q_transposed(jnp.minimum(2 * j + 2, NA_QBLOCKS - 1))
        mx_b = scores(2 * j + 1, qt_odd, sb_ref)
        emit(attend(2 * j, sa_ref, mx), 2 * j)
        qt_odd_next = q_transposed(jnp.minimum(2 * j + 3, NA_QBLOCKS))
        mx_a = scores(jnp.minimum(2 * j + 2, NA_QBLOCKS - 1), qt_even, sa_ref)
        return mx_a, qt_odd_next, attend(2 * j + 1, sb_ref, mx_b)

    def trip(t, carry):
        for u in range(NA_PAIRS_PER_TRIP):
            carry = pair(t * NA_PAIRS_PER_TRIP + u, carry)
        return carry

    first = (scores(0, q_transposed(0), sa_ref), q_transposed(1), jnp.ones((NA_VROWS, NA_TQ), F32))
    _, qt_ctx, o_late = lax.fori_loop(0, NA_QBLOCKS // (2 * NA_PAIRS_PER_TRIP), trip, first)
    emit(o_late, NA_QBLOCKS - 1)

    s_ctx = _dot(k_ref[pl.ds(SEQ, CTX), :], qt_ctx)
    p_ctx = jnp.exp2(s_ctx - jnp.max(s_ctx, axis=0, keepdims=True)).astype(BF16)
    emit(_dot(vte_ref[NA_CTX_TILE], p_ctx), NA_QBLOCKS)


def _na_attention(pe, table):
    kcol = NA_WIDTH // HEAD_DIM
    vt = pe[:, 2 * NA_WIDTH:3 * NA_WIDTH].reshape(NA_VT_TILES, NA_VT_TILE, NA_HEADS, HEAD_DIM).transpose(2, 0, 3, 1)
    return pl.pallas_call(
        _na_kernel,
        grid=(NA_HEADS,),
        in_specs=[
            pl.BlockSpec(memory_space=pltpu.SMEM),
            pl.BlockSpec((NTOK, HEAD_DIM), lambda h: (0, h)),
            pl.BlockSpec((NTOK, HEAD_DIM), lambda h: (0, kcol + h)),
            pl.BlockSpec((1, NA_VT_TILES, HEAD_DIM, NA_VT_TILE), lambda h: (h, 0, 0, 0)),
            pl.BlockSpec((1, NA_TAB, GRID_W, 2 * GRID_W), lambda h: (h, 0, 0, 0)),
        ],
        out_specs=pl.BlockSpec((NTOK, HEAD_DIM), lambda h: (0, h)),
        out_shape=jax.ShapeDtypeStruct((NTOK, NA_WIDTH), BF16),
        scratch_shapes=[
            pltpu.VMEM((NA_TK + CTX, NA_TQ), F32),
            pltpu.VMEM((NA_TK + CTX, NA_TQ), F32),
            pltpu.VMEM((NA_VT_TILES, NA_VROWS, NA_VT_TILE), BF16),
        ],
        compiler_params=_cparams("parallel"),
        name="na_attention",
    )(jnp.asarray(_NA_TILE_INDEX), pe, pe, vt, table)


S5_NCHUNK = NTOK // S5_T
S5_LAT_CHUNKS = SEQ // S5_T
S5_DROWS = S5_T * S5_GS
S5_PAIR_CH = 2 * S5_GROUP
S5_PIECES = 8
S5_PIECE_T = S5_T // S5_PIECES
S5_RE_IM_LANES = S5_GS * LANES
S5_HALF_T = S5_T // 2
S5_HALF_ROWS = S5_DROWS // 2


def _s5_fwd_chunk(j):
    return (j + S5_LAT_CHUNKS) % S5_NCHUNK


def _s5_bwd_chunk(j):
    return S5_NCHUNK - 1 - j


def _s5_kernel(uf_ref, ub_ref, bc_ref, c_ref, a_ref, yf_out, yb_out,
               b_ref, xrf_ref, xif_ref, xrb_ref, xib_ref, yfull_ref, st_ref):
    dirs = ((uf_ref, xrf_ref, xif_ref, yf_out), (ub_ref, xrb_ref, xib_ref, yb_out))

    @pl.when(pl.program_id(0) == 0)
    def _():
        st_ref[...] = jnp.zeros_like(st_ref)
        b_ref[...] = jnp.zeros_like(b_ref)
        for d in range(2):
            for s in range(S5_SLABS):
                for gs in range(S5_GS):
                    rows = slice(gs * S5_PAIR_CH, (gs + 1) * S5_PAIR_CH)
                    b_ref[d, s, rows, gs * LANES:(gs + 1) * LANES] = bc_ref[d, s, rows, 0:LANES]
                    b_ref[d, s, rows, (S5_GS + gs) * LANES:(S5_GS + gs + 1) * LANES] = (
                        bc_ref[d, s, rows, LANES:2 * LANES])

    def drive_piece(d, k):
        u_ref, xr_ref, xi_ref, _ = dirs[d]
        s, half = divmod(k, 2)
        x = _dot(u_ref[:, s * S5_SLAB_CH:(s + 1) * S5_SLAB_CH],
                 b_ref[d, s, :, half * S5_RE_IM_LANES:(half + 1) * S5_RE_IM_LANES])
        dst = xi_ref if half else xr_ref
        for gs in range(S5_GS):
            dst[s, pl.ds(gs, S5_T, stride=S5_GS), :] = x[:, gs * LANES:(gs + 1) * LANES]

    def recurrence_piece(d, k, carry):
        _, xr_ref, xi_ref, _ = dirs[d]
        ar = a_ref[d, 0]
        ai = a_ref[d, 1]
        hr, hi = carry
        for n in range(S5_PIECE_T):
            t = k * S5_PIECE_T + n
            if d == 1:
                t = S5_T - 1 - t
            rows = slice(t * S5_GS, (t + 1) * S5_GS)
            hr, hi = (ar * hr - ai * hi + xr_ref[:, rows, :],
                      ar * hi + ai * hr + xi_ref[:, rows, :])
            xr_ref[:, rows, :] = hr
            xi_ref[:, rows, :] = hi
        return hr, hi

    pairs_per_half = LANES // S5_PAIR_CH
    lane_pair = lax.broadcasted_iota(jnp.int32, (S5_HALF_T, LANES), 1) // S5_PAIR_CH

    def readout_piece(d, k):
        _, xr_ref, xi_ref, y_ref = dirs[d]
        s, th = divmod(k, 2)
        rows = slice(th * S5_HALF_ROWS, (th + 1) * S5_HALF_ROWS)
        hcat = jnp.concatenate([xr_ref[s, rows, :], xi_ref[s, rows, :]], axis=1).astype(BF16)
        yfull = _dot(hcat, c_ref[d, s])
        for hf in range(2):
            yfull_ref[d, hf] = yfull[:, hf * LANES:(hf + 1) * LANES]
        for hf in range(2):
            acc = jnp.zeros((S5_HALF_T, LANES), F32)
            for gp in range(pairs_per_half):
                gs = hf * pairs_per_half + gp
                acc = jnp.where(lane_pair == gp, yfull_ref[d, hf, pl.ds(gs, S5_HALF_T, stride=S5_GS), :], acc)
            col = s * S5_SLAB_CH + hf * LANES
            y_ref[th * S5_HALF_T:(th + 1) * S5_HALF_T, col:col + LANES] = acc

    for k in range(S5_PIECES):
        drive_piece(0, k)
    carry = (st_ref[0, 0], st_ref[0, 1])
    for k in range(S5_PIECES):
        carry = recurrence_piece(0, k, carry)
        drive_piece(1, k)
    st_ref[0, 0], st_ref[0, 1] = carry
    carry = (st_ref[1, 0], st_ref[1, 1])
    for k in range(S5_PIECES):
        carry = recurrence_piece(1, k, carry)
        readout_piece(0, k)
    st_ref[1, 0], st_ref[1, 1] = carry
    for k in range(S5_PIECES):
        readout_piece(1, k)


def _s5_tables(a_re, a_im, log_dt, b_re, b_im, c_re, c_im):
    a_re = a_re.astype(F32); a_im = a_im.astype(F32)
    dt = jnp.exp(log_dt.astype(F32))[..., None]
    mag = jnp.exp(a_re * dt)
    lb_re = mag * jnp.cos(a_im * dt)
    lb_im = mag * jnp.sin(a_im * dt)
    den = a_re * a_re + a_im * a_im
    n_re, n_im = lb_re - 1.0, lb_im
    f_re = (n_re * a_re + n_im * a_im) / den
    f_im = (n_im * a_re - n_re * a_im) / den
    bb_re = f_re[..., None] * b_re - f_im[..., None] * b_im
    bb_im = f_re[..., None] * b_im + f_im[..., None] * b_re

    def split(z):
        return z.reshape((2, S5_SLABS, S5_GS, 2) + z.shape[2:])

    eye_gl = jnp.eye(2, dtype=F32)

    def drive(bb):
        z = split(bb)
        z = jnp.einsum('dsglpc,lm->dsglcmp', z, eye_gl)
        return z.reshape(2, S5_SLABS, S5_SLAB_CH, LANES)

    bmat = jnp.concatenate([drive(bb_re), drive(bb_im)], axis=-1).astype(BF16)

    def readout(cc):
        z = split(cc.astype(F32))
        z = jnp.einsum('dshmcp,lm->dslphmc', z, eye_gl)
        return z.reshape(2, S5_SLABS, LANES, S5_SLAB_CH)

    cmat = jnp.concatenate([readout(c_re), -readout(c_im)], axis=2).astype(BF16)

    def dense(z):
        return split(z).reshape(2, S5_SLABS, S5_GS, LANES)

    avec = jnp.stack([dense(lb_re), dense(lb_im)], axis=1)
    return bmat, cmat, avec


def _s5_scan(pe, bmat, cmat, avec):
    ucol = 4 * NA_WIDTH // S5_WIDTH

    def whole(arr):
        nd = arr.ndim
        return pl.BlockSpec(arr.shape, lambda j: (0,) * nd, pipeline_mode=pl.Buffered(1))

    dense = pltpu.VMEM((S5_SLABS, S5_DROWS, LANES), F32)
    return pl.pallas_call(
        _s5_kernel,
        grid=(S5_NCHUNK,),
        in_specs=[
            pl.BlockSpec((S5_T, S5_WIDTH), lambda j: (_s5_fwd_chunk(j), ucol)),
            pl.BlockSpec((S5_T, S5_WIDTH), lambda j: (_s5_bwd_chunk(j), ucol)),
            whole(bmat), whole(cmat), whole(avec),
        ],
        out_specs=[
            pl.BlockSpec((S5_T, S5_WIDTH), lambda j: (_s5_fwd_chunk(j), 0)),
            pl.BlockSpec((S5_T, S5_WIDTH), lambda j: (_s5_bwd_chunk(j), 0)),
        ],
        out_shape=[jax.ShapeDtypeStruct((NTOK, S5_WIDTH), F32)] * 2,
        scratch_shapes=[
            pltpu.VMEM((2, S5_SLABS, S5_SLAB_CH, 2 * S5_RE_IM_LANES), BF16),
            dense, dense, dense, dense,
            pltpu.VMEM((2, 2, S5_HALF_ROWS, LANES), F32),
            pltpu.VMEM((2, 2, S5_SLABS, S5_GS, LANES), F32),
        ],
        compiler_params=_cparams("arbitrary"),
        name="s5_scan",
    )(pe, pe, bmat, cmat, avec)


def _residual_layer_norm(x, y, gt_ref, lng_ref, lnb_ref, lat):
    gt = jnp.where(lat, gt_ref[0:1, :], gt_ref[1:2, :])
    r = DN_ALPHA * x + gt * y
    mu = jnp.mean(r, axis=-1, keepdims=True)
    rc = r - mu
    var = jnp.mean(rc * rc, axis=-1, keepdims=True)
    return rc * lax.rsqrt(var + LN_EPS) * lng_ref[...] + lnb_ref[...]


def _out_even_kernel(tm, x_ref, ya_ref, ga_ref, u_ref, gb_ref, yf_ref, yb_ref, sd_ref,
                     gw_ref, gbias_ref, wo_ref, gt_ref, lng_ref, lnb_ref, o_ref):
    lat = _row_is_latent(pl.program_id(0), tm)
    za = ya_ref[...].astype(F32) * _silu(ga_ref[...].astype(F32))
    yb = u_ref[...].astype(F32) * sd_ref[...] + yf_ref[...] + yb_ref[...]
    zb = jax.nn.gelu(yb, approximate=True)
    zb = zb * jax.nn.sigmoid(_dot(zb.astype(BF16), gw_ref[...]) + gbias_ref[...])
    zb = zb * _silu(gb_ref[...].astype(F32))
    cat = jnp.concatenate([za.astype(BF16), zb.astype(BF16)], axis=1)
    y = _dot(cat, wo_ref[...])
    o_ref[...] = _residual_layer_norm(x_ref[...], y, gt_ref, lng_ref, lnb_ref, lat)


def _out_odd_kernel(tm, x_ref, o0_ref, o1_ref, g0_ref, g1_ref, wo_ref, gt_ref, lng_ref, lnb_ref, o_ref):
    lat = _row_is_latent(pl.program_id(0), tm)
    z0 = o0_ref[...].astype(F32) * _silu(g0_ref[...].astype(F32))
    z1 = o1_ref[...].astype(F32) * _silu(g1_ref[...].astype(F32))
    cat = jnp.concatenate([z0.astype(BF16), z1.astype(BF16)], axis=1)
    y = _dot(cat, wo_ref[...])
    o_ref[...] = _residual_layer_norm(x_ref[...], y, gt_ref, lng_ref, lnb_ref, lat)


def _const_spec(shape, layer=None):
    nd = len(shape)
    if layer is None:
        return pl.BlockSpec(shape, lambda i: (0,) * nd, pipeline_mode=pl.Buffered(1))
    return pl.BlockSpec((None,) + shape, lambda i: (layer,) + (0,) * nd, pipeline_mode=pl.Buffered(1))


def _out_even(xa, ya, pe, y_fwd, y_bwd, s5_d, glu_w, glu_b, w_out, layer, gt, ln_g, ln_b):
    tm = OUT_TM
    half = S5_WIDTH

    def col(c):
        return pl.BlockSpec((tm, half), lambda i: (i, c))

    return pl.pallas_call(
        functools.partial(_out_even_kernel, tm),
        grid=(NTOK // tm,),
        in_specs=[
            pl.BlockSpec((tm, D_MODEL), lambda i: (i, 0)),
            col(0),
            col(3), col(4), col(5),
            col(0),
            col(0),
            _const_spec((1, half)),
            _const_spec((half, half), layer),
            _const_spec((1, half)),
            _const_spec((D_MODEL, D_MODEL), layer),
            _const_spec((2, D_MODEL)),
            _const_spec((1, D_MODEL)),
            _const_spec((1, D_MODEL)),
        ],
        out_specs=pl.BlockSpec((tm, D_MODEL), lambda i: (i, 0)),
        out_shape=jax.ShapeDtypeStruct((NTOK, D_MODEL), F32),
        compiler_params=_cparams("parallel"),
        name="out_even",
    )(xa, ya, pe, pe, pe, y_fwd, y_bwd, s5_d.reshape(1, half), glu_w, glu_b.reshape(1, half),
      w_out, gt, ln_g.reshape(1, D_MODEL), ln_b.reshape(1, D_MODEL))


def _out_odd(xa, oa, po, w_out, layer, gt, ln_g, ln_b, last):
    tm = OUT_TM_LAST if last else OUT_TM
    nrows = SEQ if last else NTOK
    half = GQA_WIDTH // 2
    gcol = (GQA_WIDTH + 2 * GQA_KV_WIDTH) // half

    return pl.pallas_call(
        functools.partial(_out_odd_kernel, tm),
        grid=(nrows // tm,),
        in_specs=[
            pl.BlockSpec((tm, D_MODEL), lambda i: (i, 0)),
            pl.BlockSpec((tm, half), lambda i: (i, 0)),
            pl.BlockSpec((tm, half), lambda i: (i, 1)),
            pl.BlockSpec((tm, half), lambda i: (i, gcol)),
            pl.BlockSpec((tm, half), lambda i: (i, gcol + 1)),
            _const_spec((D_MODEL, D_MODEL), layer),
            _const_spec((2, D_MODEL)),
            _const_spec((1, D_MODEL)),
            _const_spec((1, D_MODEL)),
        ],
        out_specs=pl.BlockSpec((tm, D_MODEL), lambda i: (i, 0)),
        out_shape=jax.ShapeDtypeStruct((nrows, D_MODEL), F32),
        compiler_params=_cparams("parallel"),
        name="out_odd",
    )(xa, oa, oa, po, po, w_out, gt, ln_g.reshape(1, D_MODEL), ln_b.reshape(1, D_MODEL))


GQA_GROUP = GQA_HEADS // GQA_KV_HEADS
GQA_QBLOCKS = SEQ // GQA_TQ
GQA_KTILES = NTOK // GQA_TK
GQA_CTX_TILE = SEQ // GQA_TK
GQA_ONES_ROWS = 16
GQA_VROWS = HEAD_DIM + GQA_ONES_ROWS
GQA_UNROLL = 32


def _gqa_kernel(q_ref, k_ref, vt_ref, o_ref, sa_ref, sb_ref, m_ref, acc_ref, vte_ref):
    qb = pl.program_id(1)

    @pl.when(qb == 0)
    def _():
        vte_ref[:, 0:HEAD_DIM, :] = vt_ref[0]
        vte_ref[:, HEAD_DIM:GQA_VROWS, :] = jnp.ones((GQA_KTILES, GQA_ONES_ROWS, GQA_TK), BF16)

    qts = [q_ref[:, g * HEAD_DIM:(g + 1) * HEAD_DIM].astype(F32).T.astype(BF16) for g in range(GQA_GROUP)]
    is_ctx = qb == GQA_QBLOCKS

    def scores(tile, buf, g):
        k = k_ref[pl.ds(pl.multiple_of(tile * GQA_TK, GQA_TK), GQA_TK), :]
        s = _dot(k, qts[g])
        buf[g] = s
        return jnp.max(s, axis=0, keepdims=True)

    def accumulate(tile, buf, g, mx):
        m_prev = m_ref[g]
        m_cur = jnp.maximum(m_prev, mx)
        alpha = jnp.exp2(m_prev - m_cur)
        p = jnp.exp2(buf[g] - m_cur)
        acc_ref[g] = alpha * acc_ref[g] + _dot(vte_ref[tile], p.astype(BF16))
        m_ref[g] = m_cur

    def step(tile, cur, nxt, mxs):
        out = []
        for g in range(GQA_GROUP):
            out.append(scores(tile + 1, nxt, g))
            accumulate(tile, cur, g, mxs[g])
        return tuple(out)

    m_ref[...] = jnp.full_like(m_ref, MASK_VALUE)
    acc_ref[...] = jnp.zeros_like(acc_ref)
    first = jnp.where(is_ctx, GQA_CTX_TILE, 0)
    mxs = tuple(scores(first, sa_ref, g) for g in range(GQA_GROUP))

    def group(j, mxs):
        for u in range(0, GQA_UNROLL, 2):
            mxs = step(GQA_UNROLL * j + u, sa_ref, sb_ref, mxs)
            mxs = step(GQA_UNROLL * j + u + 1, sb_ref, sa_ref, mxs)
        return mxs

    mxs = lax.fori_loop(0, jnp.where(is_ctx, 0, GQA_CTX_TILE // GQA_UNROLL), group, mxs)
    for g in range(GQA_GROUP):
        accumulate(GQA_CTX_TILE, sa_ref, g, mxs[g])
        o = acc_ref[g, 0:HEAD_DIM] / acc_ref[g, HEAD_DIM:HEAD_DIM + 1]
        o_ref[:, g * HEAD_DIM:(g + 1) * HEAD_DIM] = o.T.astype(o_ref.dtype)


def _gqa_attention(po, need_ctx):
    kcol = GQA_WIDTH // HEAD_DIM
    v0 = GQA_WIDTH + GQA_KV_WIDTH
    gw = GQA_GROUP * HEAD_DIM
    nqb = GQA_QBLOCKS + (1 if need_ctx else 0)
    vt = po[:, v0:v0 + GQA_KV_WIDTH].reshape(GQA_KTILES, GQA_TK, GQA_KV_HEADS, HEAD_DIM).transpose(2, 0, 3, 1)
    return pl.pallas_call(
        _gqa_kernel,
        grid=(GQA_KV_HEADS, nqb),
        in_specs=[
            pl.BlockSpec((GQA_TQ, gw), lambda h, qb: (qb, h)),
            pl.BlockSpec((NTOK, HEAD_DIM), lambda h, qb: (0, kcol + h)),
            pl.BlockSpec((1, GQA_KTILES, HEAD_DIM, GQA_TK), lambda h, qb: (h, 0, 0, 0)),
        ],
        out_specs=pl.BlockSpec((GQA_TQ, gw), lambda h, qb: (qb, h)),
        out_shape=jax.ShapeDtypeStruct((nqb * GQA_TQ, GQA_WIDTH), BF16),
        scratch_shapes=[
            pltpu.VMEM((GQA_GROUP, GQA_TK, GQA_TQ), F32),
            pltpu.VMEM((GQA_GROUP, GQA_TK, GQA_TQ), F32),
            pltpu.VMEM((GQA_GROUP, 1, GQA_TQ), F32),
            pltpu.VMEM((GQA_GROUP, GQA_VROWS, GQA_TQ), F32),
            pltpu.VMEM((GQA_KTILES, GQA_VROWS, GQA_TK), BF16),
        ],
        compiler_params=_cparams("parallel", "arbitrary"),
        name="gqa_attention",
    )(po, po, vt)


def kernel(x, c, ctx, c_ctx, ada_w, ada_b, ln_g, ln_b, ev_w_in, ev_w_out, na_rpb, s5_a_re, s5_a_im,
           s5_log_dt, s5_b_re, s5_b_im, s5_c_re, s5_c_im, s5_d, s5_glu_w, s5_glu_b, od_w_in,
           od_w_out, q_norm_g, k_norm_g):
    assert x.shape == (1, SEQ, D_MODEL) and ctx.shape == (1, CTX, D_MODEL)
    xa = jnp.concatenate([x[0], ctx[0]], axis=0)
    mod = _ada_mod(c, c_ctx, ada_w, ada_b)
    cos2, sin2 = _rope_tables()
    ev_w_in_b = ev_w_in.astype(BF16)
    ev_w_out_b = ev_w_out.astype(BF16)
    glu_w_b = s5_glu_w.astype(BF16)
    od_w_in_b = od_w_in.astype(BF16)
    od_w_out_b = od_w_out.astype(BF16)

    for l in range(DEPTH):
        i = l // 2
        sh = mod[l, :, :D_MODEL]
        sc = mod[l, :, D_MODEL:2 * D_MODEL]
        gt = mod[l, :, 2 * D_MODEL:]
        if l % 2 == 0:
            pe = _proj_even(xa, sc, sh, ev_w_in_b, i)
            ya = _na_attention(pe, _na_bias_table(na_rpb[i]))
            bmat, cmat, avec = _s5_tables(s5_a_re[i], s5_a_im[i], s5_log_dt[i], s5_b_re[i],
                                          s5_b_im[i], s5_c_re[i], s5_c_im[i])
            y_fwd, y_bwd = _s5_scan(pe, bmat, cmat, avec)
            xa = _out_even(xa, ya, pe, y_fwd, y_bwd, s5_d[i], glu_w_b, s5_glu_b[i], ev_w_out_b, i,
                           gt, ln_g[l], ln_b[l])
        else:
            last = l == DEPTH - 1
            gains = jnp.stack([q_norm_g[i] * (ATTN_SCALE * LOG2E), k_norm_g[i]]).reshape(2, 1, HEAD_DIM)
            po = _proj_odd(xa, sc, sh, od_w_in_b, i, gains, cos2, sin2)
            oa = _gqa_attention(po, need_ctx=not last)
            xa = _out_odd(xa, oa, po, od_w_out_b, i, gt, ln_g[l], ln_b[l], last)
    return xa.reshape(1, SEQ, D_MODEL)
```
